```python
import math
import jax, jax.numpy as jnp
from jax import lax
import numpy as np

D_MODEL = 1024
BATCH = 4
SEQ = 8192
DEPTH = 4

N_MIXERS = 3
EPS = 1e-6
D_FF = 2816
CONV_WIDTH = 3
MLA_HEADS = 16
Q_LORA = 256
KV_LORA = 128
QK_NOPE = 64
QK_ROPE = 32
QK_HEAD = QK_NOPE + QK_ROPE
V_HEAD = 64
ROPE_THETA = 10000.0
Q_BLOCK = 128
SG_WIDTH = D_MODEL
SG_GROUPS = 8
SG_CHUNK = 128
N_A = (DEPTH + 2) // 3
N_B = (DEPTH + 1) // 3
N_C = DEPTH // 3

kernel_name = "hybrid_interleaved_macaron_trunk"


def rms_norm(x, g):
    xf = x.astype(jnp.float32)
    y = xf * lax.rsqrt(jnp.mean(xf * xf, axis=-1, keepdims=True) + EPS)
    return y.astype(x.dtype) * g


def swiglu(x, w_gate, w_up, w_down):
    return (jax.nn.silu(x @ w_gate) * (x @ w_up)) @ w_down


def rope(t, cos, sin):
    t1, t2 = jnp.split(t, 2, axis=-1)
    return jnp.concatenate([t1 * cos - t2 * sin, t2 * cos + t1 * sin], axis=-1)


def short_conv_mixer(x, w_in, conv_k, w_out):
    d = x.shape[-1]
    proj = x @ w_in
    b_gate, c_gate, h = proj[..., :d], proj[..., d:2 * d], proj[..., 2 * d:]
    z = c_gate * h
    conv = lax.conv_general_dilated(
        z, conv_k[:, None, :].astype(z.dtype), window_strides=(1,),
        padding=[(CONV_WIDTH - 1, 0)], dimension_numbers=("NWC", "WIO", "NWC"),
        feature_group_count=d)
    return (b_gate * conv) @ w_out


def mla_mixer(x, cos, sin, w_a, q_norm, w_uq, kv_norm, w_ukv, q_gain, k_gain, w_o):
    bsz, s, _ = x.shape
    a = x @ w_a
    c_q = a[..., :Q_LORA]
    c_kv = a[..., Q_LORA:Q_LORA + KV_LORA]
    k_pe = a[..., Q_LORA + KV_LORA:]
    q = (rms_norm(c_q, q_norm) @ w_uq).reshape(bsz, s, MLA_HEADS, QK_HEAD)
    kv = (rms_norm(c_kv, kv_norm) @ w_ukv).reshape(bsz, s, MLA_HEADS, QK_NOPE + V_HEAD)
    k_nope, v = kv[..., :QK_NOPE], kv[..., QK_NOPE:]
    k = jnp.concatenate(
        [k_nope, jnp.broadcast_to(k_pe[:, :, None, :], (bsz, s, MLA_HEADS, QK_ROPE))], axis=-1)
    q = rms_norm(q, q_gain)
    k = rms_norm(k, k_gain)
    cos_h, sin_h = cos[:, :, None, :], sin[:, :, None, :]
    q = jnp.concatenate([q[..., :QK_NOPE], rope(q[..., QK_NOPE:], cos_h, sin_h)], axis=-1)
    k = jnp.concatenate([k[..., :QK_NOPE], rope(k[..., QK_NOPE:], cos_h, sin_h)], axis=-1)
    q = q.transpose(0, 2, 1, 3)
    k = k.transpose(0, 2, 1, 3)
    v = v.transpose(0, 2, 1, 3)
    scale = QK_HEAD ** -0.5
    outs = []
    for blk in range(s // Q_BLOCK):
        q0, end = blk * Q_BLOCK, (blk + 1) * Q_BLOCK
        qb = q[:, :, q0:end]
        kb, vb = k[:, :, :end], v[:, :, :end]
        sc = jnp.einsum("bhqd,bhkd->bhqk", qb, kb).astype(jnp.float32) * scale
        mask = jnp.arange(end)[None, :] <= jnp.arange(q0, end)[:, None]
        sc = jnp.where(mask[None, None], sc, -jnp.inf)
        p = jax.nn.softmax(sc, axis=-1).astype(vb.dtype)
        outs.append(jnp.einsum("bhqk,bhkd->bhqd", p, vb))
    o = jnp.concatenate(outs, axis=2)
    o = o.transpose(0, 2, 1, 3).reshape(bsz, s, MLA_HEADS * V_HEAD)
    return o @ w_o


def spatial_gating_mixer(x, w_in, v_norm, w_s, b_s, w_out):
    bsz, s, _ = x.shape
    z = jax.nn.gelu(x @ w_in)
    u, v = z[..., :SG_WIDTH], z[..., SG_WIDTH:]
    v = rms_norm(v, v_norm)
    cg = SG_WIDTH // SG_GROUPS
    v = v.reshape(bsz, s // SG_CHUNK, SG_CHUNK, SG_GROUPS, cg)
    causal = jnp.tril(jnp.ones((SG_CHUNK, SG_CHUNK), dtype=bool))
    w_masked = jnp.where(causal[None], w_s, jnp.zeros((), w_s.dtype))
    mixed = jnp.einsum("gts,bcsgk->bctgk", w_masked, v) + b_s.T[None, None, :, :, None]
    return (u * mixed.reshape(bsz, s, SG_WIDTH)) @ w_out


def setup_inputs(seed: int = 0) -> dict:
    key = jax.random.key(seed)
    ks = jax.random.split(key, 24)

    def nrm(k, shape, fan_in):
        return jax.random.normal(k, shape, jnp.float32) * fan_in ** -0.5

    def gain(k, shape):
        return 1.0 + 0.05 * jax.random.normal(k, shape, jnp.float32)

    x = jax.random.normal(ks[0], (BATCH, SEQ, D_MODEL), jnp.float32)
    start = jax.random.randint(ks[1], (BATCH,), 0, 4096, dtype=jnp.int32)
    positions = (start[:, None] + jnp.arange(SEQ, dtype=jnp.int32)[None, :]).astype(jnp.int32)
    return {
        "x": x,
        "positions": positions,
        "norm_g": gain(ks[2], (DEPTH, 3, D_MODEL)),
        "ffn_gate": nrm(ks[3], (DEPTH, 2, D_MODEL, D_FF), D_MODEL),
        "ffn_up": nrm(ks[4], (DEPTH, 2, D_MODEL, D_FF), D_MODEL),
        "ffn_down": nrm(ks[5], (DEPTH, 2, D_FF, D_MODEL), D_FF),
        "conv_w_in": nrm(ks[6], (N_A, D_MODEL, 3 * D_MODEL), D_MODEL),
        "conv_k": nrm(ks[7], (N_A, CONV_WIDTH, D_MODEL), CONV_WIDTH),
        "conv_w_out": nrm(ks[8], (N_A, D_MODEL, D_MODEL), D_MODEL),
        "mla_w_a": nrm(ks[9], (N_B, D_MODEL, Q_LORA + KV_LORA + QK_ROPE), D_MODEL),
        "mla_q_norm": gain(ks[10], (N_B, Q_LORA)),
        "mla_w_uq": nrm(ks[11], (N_B, Q_LORA, MLA_HEADS * QK_HEAD), Q_LORA),
        "mla_kv_norm": gain(ks[12], (N_B, KV_LORA)),
        "mla_w_ukv": nrm(ks[13], (N_B, KV_LORA, MLA_HEADS * (QK_NOPE + V_HEAD)), KV_LORA),
        "mla_q_gain": gain(ks[14], (N_B, QK_HEAD)),
        "mla_k_gain": gain(ks[15], (N_B, QK_HEAD)),
        "mla_w_o": nrm(ks[16], (N_B, MLA_HEADS * V_HEAD, D_MODEL), MLA_HEADS * V_HEAD),
        "sg_w_in": nrm(ks[17], (N_C, D_MODEL, 2 * SG_WIDTH), D_MODEL),
        "sg_v_norm": gain(ks[18], (N_C, SG_WIDTH)),
        "sg_w_s": nrm(ks[19], (N_C, SG_GROUPS, SG_CHUNK, SG_CHUNK), SG_CHUNK),
        "sg_b": 1.0 + 0.1 * jax.random.normal(ks[20], (N_C, SG_GROUPS, SG_CHUNK), jnp.float32),
        "sg_w_out": nrm(ks[21], (N_C, SG_WIDTH, D_MODEL), SG_WIDTH),
    }


def reference(x, positions, norm_g, ffn_gate, ffn_up, ffn_down,
              conv_w_in, conv_k, conv_w_out,
              mla_w_a, mla_q_norm, mla_w_uq, mla_kv_norm, mla_w_ukv, mla_q_gain, mla_k_gain, mla_w_o,
              sg_w_in, sg_v_norm, sg_w_s, sg_b, sg_w_out):
    inv_freq = 1.0 / (ROPE_THETA ** (jnp.arange(0, QK_ROPE, 2, dtype=jnp.float32) / QK_ROPE))
    ang = positions.astype(jnp.float32)[..., None] * inv_freq
    cos = jnp.cos(ang).astype(x.dtype)
    sin = jnp.sin(ang).astype(x.dtype)

    ia = ib = ic = 0
    for i in range(DEPTH):
        x = x + 0.5 * swiglu(rms_norm(x, norm_g[i, 0]), ffn_gate[i, 0], ffn_up[i, 0], ffn_down[i, 0])
        hn = rms_norm(x, norm_g[i, 1])
        kind = i % N_MIXERS
        if kind == 0:
            mix = short_conv_mixer(hn, conv_w_in[ia], conv_k[ia], conv_w_out[ia])
            ia += 1
        elif kind == 1:
            mix = mla_mixer(hn, cos, sin, mla_w_a[ib], mla_q_norm[ib], mla_w_uq[ib], mla_kv_norm[ib],
                            mla_w_ukv[ib], mla_q_gain[ib], mla_k_gain[ib], mla_w_o[ib])
            ib += 1
        else:
            mix = spatial_gating_mixer(hn, sg_w_in[ic], sg_v_norm[ic], sg_w_s[ic], sg_b[ic], sg_w_out[ic])
            ic += 1
        x = x + mix
        x = x + 0.5 * swiglu(rms_norm(x, norm_g[i, 2]), ffn_gate[i, 1], ffn_up[i, 1], ffn_down[i, 1])
    return x
```

```python
import functools
import math

import jax
import jax.numpy as jnp
from jax import lax
from jax.experimental import pallas as pl
from jax.experimental.pallas import tpu as pltpu

EPS = 1e-6
ROPE_THETA = 10000.0
N_MIXERS = 3

F32 = jnp.float32
BF16 = jnp.bfloat16

V7X_LANES = 128
V7X_BF16_SUBLANES = 16
V7X_MXU_DIM = 256
V7X_VMEM_BYTES = 64 * 1024 * 1024

TOKEN_TILE = 512
ATTN_TILE = 512
FFN_CHUNK = 2 * V7X_MXU_DIM


def _vmem_limit(estimate_bytes):
    return int(min(estimate_bytes * 3 // 2 + (8 << 20), V7X_VMEM_BYTES - (6 << 20)))


def _resident(shape):
    nd = len(shape)
    return pl.BlockSpec(shape, lambda *_: (0,) * nd, pipeline_mode=pl.Buffered(1))


def _rms_rows(x, gain):
    ms = jnp.mean(x * x, axis=-1, keepdims=True)
    return x * lax.rsqrt(ms + EPS) * gain


def _ffn_kernel(x_ref, g_ref, wg_ref, wu_ref, wd_ref, o_ref, act_ref, *, chunks):
    x = x_ref[...]
    hn = _rms_rows(x, g_ref[...]).astype(BF16)
    for c0, cw in chunks:
        gate = jnp.dot(hn, wg_ref[:, c0:c0 + cw], preferred_element_type=F32)
        up = jnp.dot(hn, wu_ref[:, c0:c0 + cw], preferred_element_type=F32)
        act_ref[:, c0:c0 + cw] = (gate * jax.nn.sigmoid(gate) * up).astype(BF16)
    y = jnp.dot(act_ref[...], wd_ref[...], preferred_element_type=F32)
    o_ref[...] = x + 0.5 * y


def _ffn(x2d, gain, w_gate, w_up, w_down):
    m, d = x2d.shape
    f = w_gate.shape[1]
    tm = TOKEN_TILE
    chunks = tuple((c0, min(FFN_CHUNK, f - c0)) for c0 in range(0, f, FFN_CHUNK))
    est = 4 * tm * d * 4 + 3 * d * f * 2 + tm * f * 2 + 4 * tm * FFN_CHUNK * 4 + 2 * tm * d * 4
    return pl.pallas_call(
        functools.partial(_ffn_kernel, chunks=chunks),
        grid=(m // tm,),
        in_specs=[
            pl.BlockSpec((tm, d), lambda i: (i, 0)),
            _resident((1, d)),
            _resident((d, f)),
            _resident((d, f)),
            _resident((f, d)),
        ],
        out_specs=pl.BlockSpec((tm, d), lambda i: (i, 0)),
        out_shape=jax.ShapeDtypeStruct((m, d), F32),
        scratch_shapes=[pltpu.VMEM((tm, f), BF16)],
        compiler_params=pltpu.CompilerParams(
            dimension_semantics=("arbitrary",), vmem_limit_bytes=_vmem_limit(est)),
        name="ffn",
    )(x2d, gain, w_gate, w_up, w_down)


def _conv_kernel(x_ref, g_ref, win_ref, ck_ref, wout_ref, o_ref, z_ref, *, ts, d, halo):
    @pl.when(pl.program_id(1) == 0)
    def _():
        z_ref[0:halo, :] = jnp.zeros((halo, d), F32)

    x = x_ref[...]
    hn = _rms_rows(x, g_ref[...]).astype(BF16)
    b_gate = jnp.dot(hn, win_ref[:, 0:d], preferred_element_type=F32)
    c_gate = jnp.dot(hn, win_ref[:, d:2 * d], preferred_element_type=F32)
    h = jnp.dot(hn, win_ref[:, 2 * d:3 * d], preferred_element_type=F32)
    z = c_gate * h
    z_ref[halo:halo + ts, :] = z
    ck = ck_ref[...]
    conv = (ck[0:1, :] * z_ref[halo - 2:halo - 2 + ts, :]
            + ck[1:2, :] * z_ref[halo - 1:halo - 1 + ts, :]
            + ck[2:3, :] * z)
    z_ref[0:halo, :] = z[ts - halo:ts, :]
    y = jnp.dot((b_gate * conv).astype(BF16), wout_ref[...], preferred_element_type=F32)
    o_ref[...] = x + y


def _conv_mixer(x3d, gain, w_in, conv_k, w_out):
    b, s, d = x3d.shape
    ts = TOKEN_TILE
    halo = 8
    assert conv_k.shape[0] == 3 and halo >= conv_k.shape[0] - 1
    est = 4 * ts * d * 4 + 4 * d * d * 2 + (ts + halo) * d * 4 + 6 * ts * d * 4
    return pl.pallas_call(
        functools.partial(_conv_kernel, ts=ts, d=d, halo=halo),
        grid=(b, s // ts),
        in_specs=[
            pl.BlockSpec((None, ts, d), lambda i, j: (i, j, 0)),
            _resident((1, d)),
            _resident((d, 3 * d)),
            _resident((3, d)),
            _resident((d, d)),
        ],
        out_specs=pl.BlockSpec((None, ts, d), lambda i, j: (i, j, 0)),
        out_shape=jax.ShapeDtypeStruct((b, s, d), F32),
        scratch_shapes=[pltpu.VMEM((ts + halo, d), F32)],
        compiler_params=pltpu.CompilerParams(
            dimension_semantics=("arbitrary", "arbitrary"), vmem_limit_bytes=_vmem_limit(est)),
        name="conv_mixer",
    )(x3d, gain, w_in, conv_k, w_out)


def _sg_kernel(x_ref, g_ref, win_ref, vn_ref, ws_ref, bias_ref, wout_ref, o_ref, gated_ref,
               *, ts, e, groups, chunk):
    x = x_ref[...]
    hn = _rms_rows(x, g_ref[...]).astype(BF16)
    u = jax.nn.gelu(jnp.dot(hn, win_ref[:, 0:e], preferred_element_type=F32))
    v = jax.nn.gelu(jnp.dot(hn, win_ref[:, e:2 * e], preferred_element_type=F32))
    v = _rms_rows(v, vn_ref[...]).astype(BF16)
    cg = e // groups
    row = lax.broadcasted_iota(jnp.int32, (chunk, chunk), 0)
    col = lax.broadcasted_iota(jnp.int32, (chunk, chunk), 1)
    causal = col <= row
    for gi in range(groups):
        w_masked = jnp.where(causal, ws_ref[gi], 0.0).astype(BF16)
        for ci in range(ts // chunk):
            rows = slice(ci * chunk, (ci + 1) * chunk)
            cols = slice(gi * cg, (gi + 1) * cg)
            mixed = jnp.dot(w_masked, v[rows, cols], preferred_element_type=F32) + bias_ref[:, cols]
            gated_ref[rows, cols] = (u[rows, cols] * mixed).astype(BF16)
    y = jnp.dot(gated_ref[...], wout_ref[...], preferred_element_type=F32)
    o_ref[...] = x + y


def _sg_mixer(x2d, gain, w_in, v_norm, w_s, bias_full, w_out):
    m, d = x2d.shape
    e = w_out.shape[0]
    groups, chunk, _ = w_s.shape
    ts = TOKEN_TILE
    assert ts % chunk == 0 and (e // groups) % V7X_LANES == 0
    est = 4 * ts * d * 4 + (d * 2 * e + e * d) * 2 + groups * chunk * chunk * 4 + 6 * ts * e * 4
    return pl.pallas_call(
        functools.partial(_sg_kernel, ts=ts, e=e, groups=groups, chunk=chunk),
        grid=(m // ts,),
        in_specs=[
            pl.BlockSpec((ts, d), lambda i: (i, 0)),
            _resident((1, d)),
            _resident((d, 2 * e)),
            _resident((1, e)),
            _resident((groups, chunk, chunk)),
            _resident((chunk, e)),
            _resident((e, d)),
        ],
        out_specs=pl.BlockSpec((ts, d), lambda i: (i, 0)),
        out_shape=jax.ShapeDtypeStruct((m, d), F32),
        scratch_shapes=[pltpu.VMEM((ts, e), BF16)],
        compiler_params=pltpu.CompilerParams(
            dimension_semantics=("arbitrary",), vmem_limit_bytes=_vmem_limit(est)),
        name="sg_mixer",
    )(x2d, gain, w_in, v_norm, w_s, bias_full, w_out)


def _mla_proj_kernel(x_ref, pos_ref, g_ref, inv_freq_ref, waT_ref, qn_ref, wuqT_ref, kvn_ref, wukvT_ref,
                     qg_ref, kg_ref, qT_ref, k_ref, vT_ref,
                     *, ts, heads, q_lora, kv_lora, nope, rope, v_head, dk, dv):
    half = rope // 2
    qk_head = nope + rope
    hn = _rms_rows(x_ref[...], g_ref[...]).astype(BF16)
    aT = lax.dot_general(waT_ref[...], hn, (((1,), (1,)), ((), ())), preferred_element_type=F32)
    c_q = aT[0:q_lora, :]
    c_kv = aT[q_lora:q_lora + kv_lora, :]
    k_pe = aT[q_lora + kv_lora:q_lora + kv_lora + rope, :]

    def rms_cols(t, gain_col):
        ms = jnp.mean(t * t, axis=0, keepdims=True)
        return t * lax.rsqrt(ms + EPS) * gain_col

    cqn = rms_cols(c_q, qn_ref[...]).astype(BF16)
    ckvn = rms_cols(c_kv, kvn_ref[...]).astype(BF16)
    qT = jnp.dot(wuqT_ref[...], cqn, preferred_element_type=F32)
    kvT = jnp.dot(wukvT_ref[...], ckvn, preferred_element_type=F32)

    ang = inv_freq_ref[...] * pos_ref[...].astype(F32)
    cos = jnp.cos(ang)
    sin = jnp.sin(ang)

    def rotate(t):
        t1, t2 = t[0:half, :], t[half:rope, :]
        return t1 * cos - t2 * sin, t2 * cos + t1 * sin

    q_gain = qg_ref[...]
    k_gain = kg_ref[...]
    pe_sq = jnp.sum(k_pe * k_pe, axis=0, keepdims=True)
    zeros_q = jnp.zeros((dk - qk_head, ts), BF16)
    ones_row = (lax.broadcasted_iota(jnp.int32, (dv - v_head, ts), 0) == 0).astype(BF16)
    for h in range(heads):
        qh = qT[h * qk_head:(h + 1) * qk_head, :]
        qh = rms_cols(qh, q_gain)
        r1, r2 = rotate(qh[nope:qk_head, :])
        qT_ref[h, 0:nope, :] = qh[0:nope, :].astype(BF16)
        qT_ref[h, nope:nope + half, :] = r1.astype(BF16)
        qT_ref[h, nope + half:qk_head, :] = r2.astype(BF16)
        qT_ref[h, qk_head:dk, :] = zeros_q

        base = h * (nope + v_head)
        kn = kvT[base:base + nope, :]
        ms = (jnp.sum(kn * kn, axis=0, keepdims=True) + pe_sq) * (1.0 / qk_head)
        rs = lax.rsqrt(ms + EPS)
        kn = kn * rs * k_gain[0:nope, :]
        r1, r2 = rotate(k_pe * rs * k_gain[nope:qk_head, :])
        k_fm = jnp.concatenate([kn, r1, r2, jnp.zeros((dk - qk_head, ts), F32)], axis=0)
        k_ref[h] = k_fm.T.astype(BF16)

        vT_ref[h, 0, 0:v_head, :] = kvT[base + nope:base + nope + v_head, :].astype(BF16)
        vT_ref[h, 0, v_head:dv, :] = ones_row


def _attn_kernel(qT_ref, k_ref, vT_ref, oT_ref, *, t, v_head, log2e_scale):
    qi = pl.program_id(2)
    qT = qT_ref[...]

    def scores(j):
        kj = k_ref[pl.ds(pl.multiple_of(j * t, t), t), :]
        return jnp.dot(kj, qT, preferred_element_type=F32)

    s = scores(qi)
    k_pos = lax.broadcasted_iota(jnp.int32, (t, t), 0)
    q_pos = lax.broadcasted_iota(jnp.int32, (t, t), 1)
    s = jnp.where(k_pos <= q_pos, s, -jnp.inf)
    m = jnp.max(s, axis=0, keepdims=True)
    p = jnp.exp2(log2e_scale * s - log2e_scale * m).astype(BF16)
    acc = jnp.dot(vT_ref[qi], p, preferred_element_type=F32)

    def body(j, carry):
        m, acc = carry
        s = scores(j)
        m_new = jnp.maximum(m, jnp.max(s, axis=0, keepdims=True))
        p = jnp.exp2(log2e_scale * s - log2e_scale * m_new).astype(BF16)
        alpha = jnp.exp2(log2e_scale * m - log2e_scale * m_new)
        acc = alpha * acc + jnp.dot(vT_ref[j], p, preferred_element_type=F32)
        return m_new, acc

    m, acc = lax.fori_loop(0, qi, body, (m, acc))
    oT_ref[...] = (acc[0:v_head, :] / acc[v_head:v_head + 1, :]).astype(BF16)


def _mla_out_kernel(x_ref, oT_ref, wo_ref, o_ref):
    y = lax.dot_general(oT_ref[...], wo_ref[...], (((0,), (0,)), ((), ())), preferred_element_type=F32)
    o_ref[...] = x_ref[...] + y


def _mla_mixer(x3d, positions, gain, w_a, q_norm, w_uq, kv_norm, w_ukv, q_gain, k_gain, w_o):
    b, s, d = x3d.shape
    q_lora, kv_lora = q_norm.shape[-1], kv_norm.shape[-1]
    qk_head = q_gain.shape[-1]
    heads = w_uq.shape[1] // qk_head
    v_head = w_o.shape[0] // heads
    nope = w_ukv.shape[1] // heads - v_head
    rope = qk_head - nope
    half = rope // 2
    dk = -(-qk_head // V7X_LANES) * V7X_LANES
    dv = v_head + V7X_BF16_SUBLANES
    t = ATTN_TILE
    nt = s // t
    assert w_a.shape[1] == q_lora + kv_lora + rope and s % t == 0

    inv_freq = 1.0 / (ROPE_THETA ** (jnp.arange(0, rope, 2, dtype=F32) / rope))
    waT = w_a.T.astype(BF16)
    wuqT = w_uq.T.astype(BF16)
    wukvT = w_ukv.T.astype(BF16)

    proj_est = (2 * t * d * 4 + (w_a.size + w_uq.size + w_ukv.size) * 2
                + 2 * heads * (2 * dk + dv) * t * 2
                + (w_a.shape[1] + w_uq.shape[1] + w_ukv.shape[1]) * t * 4 * 2)
    qT, k, vT = pl.pallas_call(
        functools.partial(_mla_proj_kernel, ts=t, heads=heads, q_lora=q_lora, kv_lora=kv_lora,
                          nope=nope, rope=rope, v_head=v_head, dk=dk, dv=dv),
        grid=(b, nt),
        in_specs=[
            pl.BlockSpec((None, t, d), lambda i, j: (i, j, 0)),
            pl.BlockSpec((None, 1, t), lambda i, j: (i, 0, j)),
            _resident((1, d)),
            _resident((half, 1)),
            _resident(waT.shape),
            _resident((q_lora, 1)),
            _resident(wuqT.shape),
            _resident((kv_lora, 1)),
            _resident(wukvT.shape),
            _resident((qk_head, 1)),
            _resident((qk_head, 1)),
        ],
        out_specs=[
            pl.BlockSpec((None, heads, dk, t), lambda i, j: (i, 0, 0, j)),
            pl.BlockSpec((None, heads, t, dk), lambda i, j: (i, 0, j, 0)),
            pl.BlockSpec((None, heads, 1, dv, t), lambda i, j: (i, 0, j, 0, 0)),
        ],
        out_shape=[
            jax.ShapeDtypeStruct((b, heads, dk, s), BF16),
            jax.ShapeDtypeStruct((b, heads, s, dk), BF16),
            jax.ShapeDtypeStruct((b, heads, nt, dv, t), BF16),
        ],
        compiler_params=pltpu.CompilerParams(
            dimension_semantics=("arbitrary", "arbitrary"), vmem_limit_bytes=_vmem_limit(proj_est)),
        name="mla_proj",
    )(x3d, positions.reshape(b, 1, s), gain, inv_freq.reshape(half, 1), waT,
      q_norm.reshape(q_lora, 1), wuqT, kv_norm.reshape(kv_lora, 1), wukvT,
      q_gain.reshape(qk_head, 1), k_gain.reshape(qk_head, 1))

    attn_est = 2 * (dk * t + s * dk + nt * dv * t + v_head * t) * 2 + 6 * t * t * 4
    oT = pl.pallas_call(
        functools.partial(_attn_kernel, t=t, v_head=v_head,
                          log2e_scale=float(qk_head ** -0.5 * math.log2(math.e))),
        grid=(b, heads, nt),
        in_specs=[
            pl.BlockSpec((None, None, dk, t), lambda i, h, j: (i, h, 0, j)),
            pl.BlockSpec((None, None, s, dk), lambda i, h, j: (i, h, 0, 0)),
            pl.BlockSpec((None, None, nt, dv, t), lambda i, h, j: (i, h, 0, 0, 0)),
        ],
        out_specs=pl.BlockSpec((None, None, v_head, t), lambda i, h, j: (i, h, 0, j)),
        out_shape=jax.ShapeDtypeStruct((b, heads, v_head, s), BF16),
        compiler_params=pltpu.CompilerParams(
            dimension_semantics=("arbitrary", "arbitrary", "arbitrary"),
            vmem_limit_bytes=_vmem_limit(attn_est)),
        name="mla_attn",
    )(qT, k, vT)

    out_est = 4 * t * d * 4 + 2 * heads * v_head * t * 2 + w_o.size * 2 + 2 * t * d * 4
    return pl.pallas_call(
        _mla_out_kernel,
        grid=(b, nt),
        in_specs=[
            pl.BlockSpec((None, t, d), lambda i, j: (i, j, 0)),
            pl.BlockSpec((None, heads * v_head, t), lambda i, j: (i, 0, j)),
            _resident(w_o.shape),
        ],
        out_specs=pl.BlockSpec((None, t, d), lambda i, j: (i, j, 0)),
        out_shape=jax.ShapeDtypeStruct((b, s, d), F32),
        compiler_params=pltpu.CompilerParams(
            dimension_semantics=("arbitrary", "arbitrary"), vmem_limit_bytes=_vmem_limit(out_est)),
        name="mla_out",
    )(x3d, oT.reshape(b, heads * v_head, s), w_o.astype(BF16))


def kernel(x, positions, norm_g, ffn_gate, ffn_up, ffn_down, conv_w_in, conv_k, conv_w_out, mla_w_a, mla_q_norm, mla_w_uq, mla_kv_norm, mla_w_ukv, mla_q_gain, mla_k_gain, mla_w_o, sg_w_in, sg_v_norm, sg_w_s, sg_b, sg_w_out):
    b, s, d = x.shape
    depth = norm_g.shape[0]
    m = b * s
    assert m % TOKEN_TILE == 0 and s % TOKEN_TILE == 0

    def ffn(x3d, i, j, gi):
        out = _ffn(x3d.reshape(m, d), norm_g[i, gi].reshape(1, d), ffn_gate[i, j].astype(BF16),
                   ffn_up[i, j].astype(BF16), ffn_down[i, j].astype(BF16))
        return out.reshape(b, s, d)

    ia = ib = ic = 0
    for i in range(depth):
        x = ffn(x, i, 0, 0)
        gain = norm_g[i, 1].reshape(1, d)
        kind = i % N_MIXERS
        if kind == 0:
            x = _conv_mixer(x, gain, conv_w_in[ia].astype(BF16), conv_k[ia], conv_w_out[ia].astype(BF16))
            ia += 1
        elif kind == 1:
            x = _mla_mixer(x, positions, gain, mla_w_a[ib], mla_q_norm[ib], mla_w_uq[ib], mla_kv_norm[ib],
                           mla_w_ukv[ib], mla_q_gain[ib], mla_k_gain[ib], mla_w_o[ib])
            ib += 1
        else:
            groups, chunk = sg_w_s.shape[1], sg_w_s.shape[2]
            e = sg_w_out.shape[1]
            bias_full = jnp.repeat(sg_b[ic].T, e // groups, axis=1)
            x = _sg_mixer(x.reshape(m, d), gain, sg_w_in[ic].astype(BF16), sg_v_norm[ic].reshape(1, e),
                          sg_w_s[ic], bias_full, sg_w_out[ic].astype(BF16)).reshape(b, s, d)
            ic += 1
        x = ffn(x, i, 1, 2)
    return x
```

```python
import functools
import math

import jax
import jax.numpy as jnp
from jax import lax
from jax.experimental import pallas as pl
from jax.experimental.pallas import tpu as pltpu

EPS = 1e-6
ROPE_THETA = 10000.0
N_MIXERS = 3

F32 = jnp.float32
BF16 = jnp.bfloat16

V7X_LANES = 128
V7X_BF16_SUBLANES = 16
V7X_MXU_DIM = 256
V7X_VMEM_BYTES = 64 * 1024 * 1024

TOKEN_TILE = 512
ATTN_TILE = 512
ATTN_KEY_TILE = 256
FFN_CHUNK = 2 * V7X_MXU_DIM


def _vmem_limit(estimate_bytes):
    return int(min(estimate_bytes * 3 // 2 + (8 << 20), V7X_VMEM_BYTES - (6 << 20)))


def _resident(shape):
    nd = len(shape)
    return pl.BlockSpec(shape, lambda *_: (0,) * nd, pipeline_mode=pl.Buffered(1))


def _rms_rows(x, gain):
    ms = jnp.mean(x * x, axis=-1, keepdims=True)
    return x * lax.rsqrt(ms + EPS) * gain


def _ffn_kernel(x_ref, g_ref, wg_ref, wu_ref, wd_ref, o_ref, act_ref, *, chunks):
    x = x_ref[...]
    hn = _rms_rows(x, g_ref[...]).astype(BF16)
    for c0, cw in chunks:
        gate = jnp.dot(hn, wg_ref[:, c0:c0 + cw], preferred_element_type=F32)
        up = jnp.dot(hn, wu_ref[:, c0:c0 + cw], preferred_element_type=F32)
        act_ref[:, c0:c0 + cw] = (gate * jax.nn.sigmoid(gate) * up).astype(BF16)
    y = jnp.dot(act_ref[...], wd_ref[...], preferred_element_type=F32)
    o_ref[...] = x + 0.5 * y


def _ffn(x2d, gain, w_gate, w_up, w_down):
    m, d = x2d.shape
    f = w_gate.shape[1]
    tm = TOKEN_TILE
    chunks = tuple((c0, min(FFN_CHUNK, f - c0)) for c0 in range(0, f, FFN_CHUNK))
    est = 4 * tm * d * 4 + 3 * d * f * 2 + tm * f * 2 + 4 * tm * FFN_CHUNK * 4 + 2 * tm * d * 4
    return pl.pallas_call(
        functools.partial(_ffn_kernel, chunks=chunks),
        grid=(m // tm,),
        in_specs=[
            pl.BlockSpec((tm, d), lambda i: (i, 0)),
            _resident((1, d)),
            _resident((d, f)),
            _resident((d, f)),
            _resident((f, d)),
        ],
        out_specs=pl.BlockSpec((tm, d), lambda i: (i, 0)),
        out_shape=jax.ShapeDtypeStruct((m, d), F32),
        scratch_shapes=[pltpu.VMEM((tm, f), BF16)],
        compiler_params=pltpu.CompilerParams(
            dimension_semantics=("arbitrary",), vmem_limit_bytes=_vmem_limit(est)),
        name="ffn",
    )(x2d, gain, w_gate, w_up, w_down)


def _conv_kernel(x_ref, g_ref, win_ref, ck_ref, wout_ref, o_ref, z_ref, *, ts, d, halo):
    @pl.when(pl.program_id(1) == 0)
    def _():
        z_ref[0:halo, :] = jnp.zeros((halo, d), F32)

    x = x_ref[...]
    hn = _rms_rows(x, g_ref[...]).astype(BF16)
    b_gate = jnp.dot(hn, win_ref[:, 0:d], preferred_element_type=F32)
    c_gate = jnp.dot(hn, win_ref[:, d:2 * d], preferred_element_type=F32)
    h = jnp.dot(hn, win_ref[:, 2 * d:3 * d], preferred_element_type=F32)
    z = c_gate * h
    z_ref[halo:halo + ts, :] = z
    ck = ck_ref[...]
    conv = (ck[0:1, :] * z_ref[halo - 2:halo - 2 + ts, :]
            + ck[1:2, :] * z_ref[halo - 1:halo - 1 + ts, :]
            + ck[2:3, :] * z)
    z_ref[0:halo, :] = z[ts - halo:ts, :]
    y = jnp.dot((b_gate * conv).astype(BF16), wout_ref[...], preferred_element_type=F32)
    o_ref[...] = x + y


def _conv_mixer(x3d, gain, w_in, conv_k, w_out):
    b, s, d = x3d.shape
    ts = TOKEN_TILE
    halo = 8
    assert conv_k.shape[0] == 3 and halo >= conv_k.shape[0] - 1
    est = 4 * ts * d * 4 + 4 * d * d * 2 + (ts + halo) * d * 4 + 6 * ts * d * 4
    return pl.pallas_call(
        functools.partial(_conv_kernel, ts=ts, d=d, halo=halo),
        grid=(b, s // ts),
        in_specs=[
            pl.BlockSpec((None, ts, d), lambda i, j: (i, j, 0)),
            _resident((1, d)),
            _resident((d, 3 * d)),
            _resident((3, d)),
            _resident((d, d)),
        ],
        out_specs=pl.BlockSpec((None, ts, d), lambda i, j: (i, j, 0)),
        out_shape=jax.ShapeDtypeStruct((b, s, d), F32),
        scratch_shapes=[pltpu.VMEM((ts + halo, d), F32)],
        compiler_params=pltpu.CompilerParams(
            dimension_semantics=("arbitrary", "arbitrary"), vmem_limit_bytes=_vmem_limit(est)),
        name="conv_mixer",
    )(x3d, gain, w_in, conv_k, w_out)


def _sg_kernel(x_ref, g_ref, win_ref, vn_ref, ws_ref, bias_ref, wout_ref, o_ref, gated_ref,
               *, ts, e, groups, chunk):
    x = x_ref[...]
    hn = _rms_rows(x, g_ref[...]).astype(BF16)
    u = jax.nn.gelu(jnp.dot(hn, win_ref[:, 0:e], preferred_element_type=F32))
    v = jax.nn.gelu(jnp.dot(hn, win_ref[:, e:2 * e], preferred_element_type=F32))
    v = _rms_rows(v, vn_ref[...]).astype(BF16)
    cg = e // groups
    row = lax.broadcasted_iota(jnp.int32, (chunk, chunk), 0)
    col = lax.broadcasted_iota(jnp.int32, (chunk, chunk), 1)
    causal = col <= row
    for gi in range(groups):
        w_masked = jnp.where(causal, ws_ref[gi], 0.0).astype(BF16)
        for ci in range(ts // chunk):
            rows = slice(ci * chunk, (ci + 1) * chunk)
            cols = slice(gi * cg, (gi + 1) * cg)
            mixed = jnp.dot(w_masked, v[rows, cols], preferred_element_type=F32) + bias_ref[:, cols]
            gated_ref[rows, cols] = (u[rows, cols] * mixed).astype(BF16)
    y = jnp.dot(gated_ref[...], wout_ref[...], preferred_element_type=F32)
    o_ref[...] = x + y


def _sg_mixer(x2d, gain, w_in, v_norm, w_s, bias_full, w_out):
    m, d = x2d.shape
    e = w_out.shape[0]
    groups, chunk, _ = w_s.shape
    ts = TOKEN_TILE
    assert ts % chunk == 0 and (e // groups) % V7X_LANES == 0
    est = 4 * ts * d * 4 + (d * 2 * e + e * d) * 2 + groups * chunk * chunk * 4 + 6 * ts * e * 4
    return pl.pallas_call(
        functools.partial(_sg_kernel, ts=ts, e=e, groups=groups, chunk=chunk),
        grid=(m // ts,),
        in_specs=[
            pl.BlockSpec((ts, d), lambda i: (i, 0)),
            _resident((1, d)),
            _resident((d, 2 * e)),
            _resident((1, e)),
            _resident((groups, chunk, chunk)),
            _resident((chunk, e)),
            _resident((e, d)),
        ],
        out_specs=pl.BlockSpec((ts, d), lambda i: (i, 0)),
        out_shape=jax.ShapeDtypeStruct((m, d), F32),
        scratch_shapes=[pltpu.VMEM((ts, e), BF16)],
        compiler_params=pltpu.CompilerParams(
            dimension_semantics=("arbitrary",), vmem_limit_bytes=_vmem_limit(est)),
        name="sg_mixer",
    )(x2d, gain, w_in, v_norm, w_s, bias_full, w_out)


def _mla_proj_kernel(x_ref, pos_ref, g_ref, inv_freq_ref, waT_ref, qn_ref, wuqT_ref, kvn_ref, wukvT_ref,
                     qg_ref, kg_ref, qT_ref, k_ref, vT_ref,
                     *, ts, kt, heads, q_lora, kv_lora, nope, rope, v_head, dk, dv, q_scale):
    half = rope // 2
    qk_head = nope + rope
    hn = _rms_rows(x_ref[...], g_ref[...]).astype(BF16)
    aT = lax.dot_general(waT_ref[...], hn, (((1,), (1,)), ((), ())), preferred_element_type=F32)
    c_q = aT[0:q_lora, :]
    c_kv = aT[q_lora:q_lora + kv_lora, :]
    k_pe = aT[q_lora + kv_lora:q_lora + kv_lora + rope, :]

    def rms_cols(t, gain_col):
        ms = jnp.mean(t * t, axis=0, keepdims=True)
        return t * lax.rsqrt(ms + EPS) * gain_col

    cqn = rms_cols(c_q, qn_ref[...]).astype(BF16)
    ckvn = rms_cols(c_kv, kvn_ref[...]).astype(BF16)
    qT = jnp.dot(wuqT_ref[...], cqn, preferred_element_type=F32)
    kvT = jnp.dot(wukvT_ref[...], ckvn, preferred_element_type=F32)

    ang = inv_freq_ref[...] * pos_ref[...].astype(F32)
    cos = jnp.cos(ang)
    sin = jnp.sin(ang)

    def rotate(t):
        t1, t2 = t[0:half, :], t[half:rope, :]
        return t1 * cos - t2 * sin, t2 * cos + t1 * sin

    q_gain = qg_ref[...] * q_scale
    k_gain = kg_ref[...]
    pe_sq = jnp.sum(k_pe * k_pe, axis=0, keepdims=True)
    zeros_q = jnp.zeros((dk - qk_head, ts), BF16)
    ones_row = (lax.broadcasted_iota(jnp.int32, (dv - v_head, kt), 0) == 0).astype(BF16)
    for h in range(heads):
        qh = qT[h * qk_head:(h + 1) * qk_head, :]
        qh = rms_cols(qh, q_gain)
        r1, r2 = rotate(qh[nope:qk_head, :])
        qT_ref[h, 0:nope, :] = qh[0:nope, :].astype(BF16)
        qT_ref[h, nope:nope + half, :] = r1.astype(BF16)
        qT_ref[h, nope + half:qk_head, :] = r2.astype(BF16)
        qT_ref[h, qk_head:dk, :] = zeros_q

        base = h * (nope + v_head)
        kn = kvT[base:base + nope, :]
        ms = (jnp.sum(kn * kn, axis=0, keepdims=True) + pe_sq) * (1.0 / qk_head)
        rs = lax.rsqrt(ms + EPS)
        kn = kn * rs * k_gain[0:nope, :]
        r1, r2 = rotate(k_pe * rs * k_gain[nope:qk_head, :])
        k_fm = jnp.concatenate([kn, r1, r2, jnp.zeros((dk - qk_head, ts), F32)], axis=0)
        k_ref[h] = k_fm.T.astype(BF16)

        for n in range(ts // kt):
            cols = slice(n * kt, (n + 1) * kt)
            vT_ref[h, n, 0:v_head, :] = kvT[base + nope:base + nope + v_head, cols].astype(BF16)
            vT_ref[h, n, v_head:dv, :] = ones_row


def _attn_kernel(qT_ref, k_ref, vT_ref, oT_ref, s_ref, p_ref, *, t, kt, v_head):
    n_sub = t // kt
    qi = pl.program_id(2)
    qT = qT_ref[...]

    def scores(j, q_lo=0):
        start = j * kt if isinstance(j, int) else pl.multiple_of(j * kt, kt)
        kj = k_ref[pl.ds(start, kt), :]
        return jnp.dot(kj, qT[:, q_lo:], preferred_element_type=F32)

    def softmax(s, m):
        m_new = jnp.maximum(m, jnp.max(s, axis=0, keepdims=True))
        p = jnp.exp2(s - m_new).astype(BF16)
        alpha = jnp.exp2(m - m_new)
        return m_new, alpha, p

    def pv(j, p):
        return jnp.dot(vT_ref[j], p, preferred_element_type=F32)

    def update(s, j, m, acc):
        m_new, alpha, p = softmax(s, m)
        return m_new, alpha * acc + pv(j, p)

    def flush(g, alphas, acc):
        for n in range(n_sub):
            acc = alphas[n] * acc + pv(jnp.maximum((g - 1) * n_sub + n, 0), p_ref[n])
        return acc

    def group(g, rd, wr, m, alphas, acc):
        acc = flush(g, alphas, acc)
        for n in range(n_sub):
            s_ref[wr + n] = scores((g + 1) * n_sub + n)
        alphas = []
        for n in range(n_sub):
            m, alpha, p = softmax(s_ref[rd + n], m)
            p_ref[n] = p
            alphas.append(alpha)
        return m, tuple(alphas), acc

    def finish(rd, m, alphas, acc):
        j0 = qi * n_sub
        acc = flush(qi, alphas, acc)
        for n in range(n_sub):
            q_lo = n * kt
            s = s_ref[rd + n, :, q_lo:]
            k_pos = lax.broadcasted_iota(jnp.int32, s.shape, 0)
            q_pos = lax.broadcasted_iota(jnp.int32, s.shape, 1)
            s = jnp.where(k_pos <= q_pos, s, -jnp.inf)
            m_hi, acc_hi = update(s, j0 + n, m[:, q_lo:], acc[:, q_lo:])
            if n == 0:
                m, acc = m_hi, acc_hi
            else:
                m = jnp.concatenate([m[:, :q_lo], m_hi], axis=1)
                acc = jnp.concatenate([acc[:, :q_lo], acc_hi], axis=1)
        oT_ref[...] = (acc[0:v_head, :] / acc[v_head:v_head + 1, :]).astype(BF16)

    def pair(gg, carry):
        carry = group(2 * gg, 0, n_sub, *carry)
        return group(2 * gg + 1, n_sub, 0, *carry)

    for n in range(n_sub):
        s_ref[n] = scores(n)
    p_ref[...] = jnp.zeros(p_ref.shape, BF16)
    m = jnp.full((1, t), -jnp.inf, F32)
    alphas = (jnp.ones((1, t), F32),) * n_sub
    acc = jnp.zeros((vT_ref.shape[1], t), F32)
    carry = lax.fori_loop(0, qi // 2, pair, (m, alphas, acc))

    @pl.when(qi % 2 == 0)
    def _():
        finish(0, *carry)

    @pl.when(qi % 2 == 1)
    def _():
        finish(n_sub, *group(qi - 1, 0, n_sub, *carry))


def _mla_out_kernel(x_ref, oT_ref, wo_ref, o_ref):
    y = lax.dot_general(oT_ref[...], wo_ref[...], (((0,), (0,)), ((), ())), preferred_element_type=F32)
    o_ref[...] = x_ref[...] + y


def _mla_mixer(x3d, positions, gain, w_a, q_norm, w_uq, kv_norm, w_ukv, q_gain, k_gain, w_o):
    b, s, d = x3d.shape
    q_lora, kv_lora = q_norm.shape[-1], kv_norm.shape[-1]
    qk_head = q_gain.shape[-1]
    heads = w_uq.shape[1] // qk_head
    v_head = w_o.shape[0] // heads
    nope = w_ukv.shape[1] // heads - v_head
    rope = qk_head - nope
    half = rope // 2
    dk = -(-qk_head // V7X_LANES) * V7X_LANES
    dv = v_head + V7X_BF16_SUBLANES
    t, kt = ATTN_TILE, ATTN_KEY_TILE
    nt, nkt = s // t, s // kt
    assert w_a.shape[1] == q_lora + kv_lora + rope and s % t == 0 and (t // kt) % 2 == 0

    inv_freq = 1.0 / (ROPE_THETA ** (jnp.arange(0, rope, 2, dtype=F32) / rope))
    waT = w_a.T.astype(BF16)
    wuqT = w_uq.T.astype(BF16)
    wukvT = w_ukv.T.astype(BF16)

    proj_est = (2 * t * d * 4 + (w_a.size + w_uq.size + w_ukv.size) * 2
                + 2 * heads * (2 * dk + dv) * t * 2
                + (w_a.shape[1] + w_uq.shape[1] + w_ukv.shape[1]) * t * 4 * 2)
    qT, k, vT = pl.pallas_call(
        functools.partial(_mla_proj_kernel, ts=t, kt=kt, heads=heads, q_lora=q_lora, kv_lora=kv_lora,
                          nope=nope, rope=rope, v_head=v_head, dk=dk, dv=dv,
                          q_scale=float(qk_head ** -0.5 * math.log2(math.e))),
        grid=(b, nt),
        in_specs=[
            pl.BlockSpec((None, t, d), lambda i, j: (i, j, 0)),
            pl.BlockSpec((None, 1, t), lambda i, j: (i, 0, j)),
            _resident((1, d)),
            _resident((half, 1)),
            _resident(waT.shape),
            _resident((q_lora, 1)),
            _resident(wuqT.shape),
            _resident((kv_lora, 1)),
            _resident(wukvT.shape),
            _resident((qk_head, 1)),
            _resident((qk_head, 1)),
        ],
        out_specs=[
            pl.BlockSpec((None, heads, dk, t), lambda i, j: (i, 0, 0, j)),
            pl.BlockSpec((None, heads, t, dk), lambda i, j: (i, 0, j, 0)),
            pl.BlockSpec((None, heads, t // kt, dv, kt), lambda i, j: (i, 0, j, 0, 0)),
        ],
        out_shape=[
            jax.ShapeDtypeStruct((b, heads, dk, s), BF16),
            jax.ShapeDtypeStruct((b, heads, s, dk), BF16),
            jax.ShapeDtypeStruct((b, heads, nkt, dv, kt), BF16),
        ],
        compiler_params=pltpu.CompilerParams(
            dimension_semantics=("arbitrary", "arbitrary"), vmem_limit_bytes=_vmem_limit(proj_est)),
        name="mla_proj",
    )(x3d, positions.reshape(b, 1, s), gain, inv_freq.reshape(half, 1), waT,
      q_norm.reshape(q_lora, 1), wuqT, kv_norm.reshape(kv_lora, 1), wukvT,
      q_gain.reshape(qk_head, 1), k_gain.reshape(qk_head, 1))

    attn_est = 2 * (dk * t + s * dk + nkt * dv * kt + v_head * t) * 2 + 8 * kt * t * 4
    oT = pl.pallas_call(
        functools.partial(_attn_kernel, t=t, kt=kt, v_head=v_head),
        grid=(b, heads, nt),
        in_specs=[
            pl.BlockSpec((None, None, dk, t), lambda i, h, j: (i, h, 0, j)),
            pl.BlockSpec((None, None, s, dk), lambda i, h, j: (i, h, 0, 0)),
            pl.BlockSpec((None, None, nkt, dv, kt), lambda i, h, j: (i, h, 0, 0, 0)),
        ],
        out_specs=pl.BlockSpec((None, None, v_head, t), lambda i, h, j: (i, h, 0, j)),
        out_shape=jax.ShapeDtypeStruct((b, heads, v_head, s), BF16),
        scratch_shapes=[pltpu.VMEM((2 * (t // kt), kt, t), F32), pltpu.VMEM((t // kt, kt, t), BF16)],
        compiler_params=pltpu.CompilerParams(
            dimension_semantics=("arbitrary", "arbitrary", "arbitrary"),
            vmem_limit_bytes=_vmem_limit(attn_est)),
        name="mla_attn",
    )(qT, k, vT)

    out_est = 4 * t * d * 4 + 2 * heads * v_head * t * 2 + w_o.size * 2 + 2 * t * d * 4
    return pl.pallas_call(
        _mla_out_kernel,
        grid=(b, nt),
        in_specs=[
            pl.BlockSpec((None, t, d), lambda i, j: (i, j, 0)),
            pl.BlockSpec((None, heads * v_head, t), lambda i, j: (i, 0, j)),
            _resident(w_o.shape),
        ],
        out_specs=pl.BlockSpec((None, t, d), lambda i, j: (i, j, 0)),
        out_shape=jax.ShapeDtypeStruct((b, s, d), F32),
        compiler_params=pltpu.CompilerParams(
            dimension_semantics=("arbitrary", "arbitrary"), vmem_limit_bytes=_vmem_limit(out_est)),
        name="mla_out",
    )(x3d, oT.reshape(b, heads * v_head, s), w_o.astype(BF16))


def kernel(x, positions, norm_g, ffn_gate, ffn_up, ffn_down, conv_w_in, conv_k, conv_w_out, mla_w_a, mla_q_norm, mla_w_uq, mla_kv_norm, mla_w_ukv, mla_q_gain, mla_k_gain, mla_w_o, sg_w_in, sg_v_norm, sg_w_s, sg_b, sg_w_out):
    b, s, d = x.shape
    depth = norm_g.shape[0]
    m = b * s
    assert m % TOKEN_TILE == 0 and s % TOKEN_TILE == 0

    def ffn(x3d, i, j, gi):
        out = _ffn(x3d.reshape(m, d), norm_g[i, gi].reshape(1, d), ffn_gate[i, j].astype(BF16),
                   ffn_up[i, j].astype(BF16), ffn_down[i, j].astype(BF16))
        return out.reshape(b, s, d)

    ia = ib = ic = 0
    for i in range(depth):
        x = ffn(x, i, 0, 0)
        gain = norm_g[i, 1].reshape(1, d)
        kind = i % N_MIXERS
        if kind == 0:
            x = _conv_mixer(x, gain, conv_w_in[ia].astype(BF16), conv_k[ia], conv_w_out[ia].astype(BF16))
            ia += 1
        elif kind == 1:
            x = _mla_mixer(x, positions, gain, mla_w_a[ib], mla_q_norm[ib], mla_w_uq[ib], mla_kv_norm[ib],
                           mla_w_ukv[ib], mla_q_gain[ib], mla_k_gain[ib], mla_w_o[ib])
            ib += 1
        else:
            groups, chunk = sg_w_s.shape[1], sg_w_s.shape[2]
            e = sg_w_out.shape[1]
            bias_full = jnp.repeat(sg_b[ic].T, e // groups, axis=1)
            x = _sg_mixer(x.reshape(m, d), gain, sg_w_in[ic].astype(BF16), sg_v_norm[ic].reshape(1, e),
                          sg_w_s[ic], bias_full, sg_w_out[ic].astype(BF16)).reshape(b, s, d)
            ic += 1
        x = ffn(x, i, 1, 2)
    return x
```

```python
import functools
import math

import jax
import jax.numpy as jnp
from jax import lax
from jax.experimental import pallas as pl
from jax.experimental.pallas import tpu as pltpu

EPS = 1e-6
ROPE_THETA = 10000.0
N_MIXERS = 3

F32 = jnp.float32
BF16 = jnp.bfloat16

V7X_LANES = 128
V7X_BF16_SUBLANES = 16
V7X_MXU_DIM = 256
V7X_VMEM_BYTES = 64 * 1024 * 1024

TOKEN_TILE = 512
ATTN_TILE = 512
ATTN_KEY_TILE = 256
FFN_CHUNK = 2 * V7X_MXU_DIM


def _vmem_limit(estimate_bytes):
    return int(min(estimate_bytes * 3 // 2 + (8 << 20), V7X_VMEM_BYTES - (6 << 20)))


def _resident(shape):
    nd = len(shape)
    return pl.BlockSpec(shape, lambda *_: (0,) * nd, pipeline_mode=pl.Buffered(1))


def _rms_rows(x, gain):
    ms = jnp.mean(x * x, axis=-1, keepdims=True)
    return x * lax.rsqrt(ms + EPS) * gain


def _ffn_kernel(x_ref, g_ref, wg_ref, wu_ref, wd_ref, o_ref, act_ref, *, chunks):
    x = x_ref[...]
    hn = _rms_rows(x, g_ref[...]).astype(BF16)
    for c0, cw in chunks:
        gate = jnp.dot(hn, wg_ref[:, c0:c0 + cw], preferred_element_type=F32)
        up = jnp.dot(hn, wu_ref[:, c0:c0 + cw], preferred_element_type=F32)
        act_ref[:, c0:c0 + cw] = (gate * jax.nn.sigmoid(gate) * up).astype(BF16)
    y = jnp.dot(act_ref[...], wd_ref[...], preferred_element_type=F32)
    o_ref[...] = x + 0.5 * y


def _ffn(x2d, gain, w_gate, w_up, w_down):
    m, d = x2d.shape
    f = w_gate.shape[1]
    tm = TOKEN_TILE
    chunks = tuple((c0, min(FFN_CHUNK, f - c0)) for c0 in range(0, f, FFN_CHUNK))
    est = 4 * tm * d * 4 + 3 * d * f * 2 + tm * f * 2 + 4 * tm * FFN_CHUNK * 4 + 2 * tm * d * 4
    return pl.pallas_call(
        functools.partial(_ffn_kernel, chunks=chunks),
        grid=(m // tm,),
        in_specs=[
            pl.BlockSpec((tm, d), lambda i: (i, 0)),
            _resident((1, d)),
            _resident((d, f)),
            _resident((d, f)),
            _resident((f, d)),
        ],
        out_specs=pl.BlockSpec((tm, d), lambda i: (i, 0)),
        out_shape=jax.ShapeDtypeStruct((m, d), F32),
        scratch_shapes=[pltpu.VMEM((tm, f), BF16)],
        compiler_params=pltpu.CompilerParams(
            dimension_semantics=("arbitrary",), vmem_limit_bytes=_vmem_limit(est)),
        name="ffn",
    )(x2d, gain, w_gate, w_up, w_down)


def _conv_kernel(x_ref, g_ref, win_ref, ck_ref, wout_ref, o_ref, z_ref, *, ts, d, halo):
    @pl.when(pl.program_id(1) == 0)
    def _():
        z_ref[0:halo, :] = jnp.zeros((halo, d), F32)

    x = x_ref[...]
    hn = _rms_rows(x, g_ref[...]).astype(BF16)
    b_gate = jnp.dot(hn, win_ref[:, 0:d], preferred_element_type=F32)
    c_gate = jnp.dot(hn, win_ref[:, d:2 * d], preferred_element_type=F32)
    h = jnp.dot(hn, win_ref[:, 2 * d:3 * d], preferred_element_type=F32)
    z = c_gate * h
    z_ref[halo:halo + ts, :] = z
    ck = ck_ref[...]
    conv = (ck[0:1, :] * z_ref[halo - 2:halo - 2 + ts, :]
            + ck[1:2, :] * z_ref[halo - 1:halo - 1 + ts, :]
            + ck[2:3, :] * z)
    z_ref[0:halo, :] = z[ts - halo:ts, :]
    y = jnp.dot((b_gate * conv).astype(BF16), wout_ref[...], preferred_element_type=F32)
    o_ref[...] = x + y


def _conv_mixer(x3d, gain, w_in, conv_k, w_out):
    b, s, d = x3d.shape
    ts = TOKEN_TILE
    halo = 8
    assert conv_k.shape[0] == 3 and halo >= conv_k.shape[0] - 1
    est = 4 * ts * d * 4 + 4 * d * d * 2 + (ts + halo) * d * 4 + 6 * ts * d * 4
    return pl.pallas_call(
        functools.partial(_conv_kernel, ts=ts, d=d, halo=halo),
        grid=(b, s // ts),
        in_specs=[
            pl.BlockSpec((None, ts, d), lambda i, j: (i, j, 0)),
            _resident((1, d)),
            _resident((d, 3 * d)),
            _resident((3, d)),
            _resident((d, d)),
        ],
        out_specs=pl.BlockSpec((None, ts, d), lambda i, j: (i, j, 0)),
        out_shape=jax.ShapeDtypeStruct((b, s, d), F32),
        scratch_shapes=[pltpu.VMEM((ts + halo, d), F32)],
        compiler_params=pltpu.CompilerParams(
            dimension_semantics=("arbitrary", "arbitrary"), vmem_limit_bytes=_vmem_limit(est)),
        name="conv_mixer",
    )(x3d, gain, w_in, conv_k, w_out)


def _sg_kernel(x_ref, g_ref, win_ref, vn_ref, ws_ref, bias_ref, wout_ref, o_ref, gated_ref,
               *, ts, e, groups, chunk):
    x = x_ref[...]
    hn = _rms_rows(x, g_ref[...]).astype(BF16)
    u = jax.nn.gelu(jnp.dot(hn, win_ref[:, 0:e], preferred_element_type=F32))
    v = jax.nn.gelu(jnp.dot(hn, win_ref[:, e:2 * e], preferred_element_type=F32))
    v = _rms_rows(v, vn_ref[...]).astype(BF16)
    cg = e // groups
    row = lax.broadcasted_iota(jnp.int32, (chunk, chunk), 0)
    col = lax.broadcasted_iota(jnp.int32, (chunk, chunk), 1)
    causal = col <= row
    for gi in range(groups):
        w_masked = jnp.where(causal, ws_ref[gi], 0.0).astype(BF16)
        for ci in range(ts // chunk):
            rows = slice(ci * chunk, (ci + 1) * chunk)
            cols = slice(gi * cg, (gi + 1) * cg)
            mixed = jnp.dot(w_masked, v[rows, cols], preferred_element_type=F32) + bias_ref[:, cols]
            gated_ref[rows, cols] = (u[rows, cols] * mixed).astype(BF16)
    y = jnp.dot(gated_ref[...], wout_ref[...], preferred_element_type=F32)
    o_ref[...] = x + y


def _sg_mixer(x2d, gain, w_in, v_norm, w_s, bias_full, w_out):
    m, d = x2d.shape
    e = w_out.shape[0]
    groups, chunk, _ = w_s.shape
    ts = TOKEN_TILE
    assert ts % chunk == 0 and (e // groups) % V7X_LANES == 0
    est = 4 * ts * d * 4 + (d * 2 * e + e * d) * 2 + groups * chunk * chunk * 4 + 6 * ts * e * 4
    return pl.pallas_call(
        functools.partial(_sg_kernel, ts=ts, e=e, groups=groups, chunk=chunk),
        grid=(m // ts,),
        in_specs=[
            pl.BlockSpec((ts, d), lambda i: (i, 0)),
            _resident((1, d)),
            _resident((d, 2 * e)),
            _resident((1, e)),
            _resident((groups, chunk, chunk)),
            _resident((chunk, e)),
            _resident((e, d)),
        ],
        out_specs=pl.BlockSpec((ts, d), lambda i: (i, 0)),
        out_shape=jax.ShapeDtypeStruct((m, d), F32),
        scratch_shapes=[pltpu.VMEM((ts, e), BF16)],
        compiler_params=pltpu.CompilerParams(
            dimension_semantics=("arbitrary",), vmem_limit_bytes=_vmem_limit(est)),
        name="sg_mixer",
    )(x2d, gain, w_in, v_norm, w_s, bias_full, w_out)


def _mla_proj_kernel(x_ref, pos_ref, g_ref, inv_freq_ref, waT_ref, qn_ref, wuqT_ref, kvn_ref, wukvT_ref,
                     qg_ref, kg_ref, qT_ref, k_ref, vT_ref,
                     *, ts, heads, q_lora, kv_lora, nope, rope, v_head, dk, dv, q_scale):
    half = rope // 2
    qk_head = nope + rope
    hn = _rms_rows(x_ref[...], g_ref[...]).astype(BF16)
    aT = lax.dot_general(waT_ref[...], hn, (((1,), (1,)), ((), ())), preferred_element_type=F32)
    c_q = aT[0:q_lora, :]
    c_kv = aT[q_lora:q_lora + kv_lora, :]
    k_pe = aT[q_lora + kv_lora:q_lora + kv_lora + rope, :]

    def rms_cols(t, gain_col):
        ms = jnp.mean(t * t, axis=0, keepdims=True)
        return t * lax.rsqrt(ms + EPS) * gain_col

    cqn = rms_cols(c_q, qn_ref[...]).astype(BF16)
    ckvn = rms_cols(c_kv, kvn_ref[...]).astype(BF16)
    qT = jnp.dot(wuqT_ref[...], cqn, preferred_element_type=F32)
    kvT = jnp.dot(wukvT_ref[...], ckvn, preferred_element_type=F32)

    ang = inv_freq_ref[...] * pos_ref[...].astype(F32)
    cos = jnp.cos(ang)
    sin = jnp.sin(ang)

    def rotate(t):
        t1, t2 = t[0:half, :], t[half:rope, :]
        return t1 * cos - t2 * sin, t2 * cos + t1 * sin

    q_gain = qg_ref[...] * q_scale
    k_gain = kg_ref[...]
    pe_sq = jnp.sum(k_pe * k_pe, axis=0, keepdims=True)
    zeros_q = jnp.zeros((dk - qk_head, ts), BF16)
    ones_row = (lax.broadcasted_iota(jnp.int32, (dv - v_head, ts), 0) == 0).astype(BF16)
    for h in range(heads):
        qh = qT[h * qk_head:(h + 1) * qk_head, :]
        qh = rms_cols(qh, q_gain)
        r1, r2 = rotate(qh[nope:qk_head, :])
        qT_ref[h, 0:nope, :] = qh[0:nope, :].astype(BF16)
        qT_ref[h, nope:nope + half, :] = r1.astype(BF16)
        qT_ref[h, nope + half:qk_head, :] = r2.astype(BF16)
        qT_ref[h, qk_head:dk, :] = zeros_q

        base = h * (nope + v_head)
        kn = kvT[base:base + nope, :]
        ms = (jnp.sum(kn * kn, axis=0, keepdims=True) + pe_sq) * (1.0 / qk_head)
        rs = lax.rsqrt(ms + EPS)
        kn = kn * rs * k_gain[0:nope, :]
        r1, r2 = rotate(k_pe * rs * k_gain[nope:qk_head, :])
        k_fm = jnp.concatenate([kn, r1, r2, jnp.zeros((dk - qk_head, ts), F32)], axis=0)
        k_ref[h] = k_fm.T.astype(BF16)

        vT_ref[h, 0:v_head, :] = kvT[base + nope:base + nope + v_head, :].astype(BF16)
        vT_ref[h, v_head:dv, :] = ones_row


def _attn_kernel(qT_ref, k_ref, vT_ref, oT_ref, s_ref, mx_ref, *, t, kt, v_head):
    n_sub = t // kt
    n_tiles = qT_ref.shape[0]

    def produce(qT, g, wr):
        mx = None
        for n in range(n_sub):
            start = (g * n_sub + n) * kt
            if not isinstance(start, int):
                start = pl.multiple_of(start, kt)
            sc = jnp.dot(k_ref[pl.ds(start, kt), :], qT, preferred_element_type=F32)
            s_ref[wr + n] = sc
            sub_max = jnp.max(sc, axis=0, keepdims=True)
            mx = sub_max if mx is None else jnp.maximum(mx, sub_max)
        return mx

    mx_ref[...] = produce(qT_ref[0], 0, 0)

    def q_tile(qi, _):
        _attn_query_tile(qi, qT_ref, vT_ref, oT_ref, s_ref, mx_ref, produce,
                         t=t, kt=kt, v_head=v_head, n_sub=n_sub, n_tiles=n_tiles)
        return 0

    lax.fori_loop(0, n_tiles, q_tile, 0)


def _attn_query_tile(qi, qT_ref, vT_ref, oT_ref, s_ref, mx_ref, produce, *, t, kt, v_head, n_sub, n_tiles):
    qT = qT_ref[qi]

    def group(g, rd, wr, m, mx, acc):
        mx_next = produce(qT, g + 1, wr)
        m_new = jnp.maximum(m, mx)
        alpha = jnp.exp2(m - m_new)
        p = jnp.concatenate([jnp.exp2(s_ref[rd + n] - m_new).astype(BF16) for n in range(n_sub)], axis=0)
        acc = alpha * acc + jnp.dot(vT_ref[g], p, preferred_element_type=F32)
        return m_new, mx_next, acc

    def finish(rd, m, mx, acc):
        del mx
        masked = []
        for n in range(n_sub):
            s = s_ref[rd + n, :, n * kt:]
            k_pos = lax.broadcasted_iota(jnp.int32, s.shape, 0)
            q_pos = lax.broadcasted_iota(jnp.int32, s.shape, 1)
            masked.append(jnp.where(k_pos <= q_pos, s, -jnp.inf))
        vT = vT_ref[qi]
        mx_ref[...] = produce(qT_ref[jnp.minimum(qi + 1, n_tiles - 1)], 0, 0)
        for n in range(n_sub):
            q_lo = n * kt
            s, m_old, acc_old = masked[n], m[:, q_lo:], acc[:, q_lo:]
            m_hi = jnp.maximum(m_old, jnp.max(s, axis=0, keepdims=True))
            p = jnp.exp2(s - m_hi).astype(BF16)
            acc_hi = jnp.exp2(m_old - m_hi) * acc_old + jnp.dot(
                vT[:, q_lo:q_lo + kt], p, preferred_element_type=F32)
            if n == 0:
                m, acc = m_hi, acc_hi
            else:
                m = jnp.concatenate([m[:, :q_lo], m_hi], axis=1)
                acc = jnp.concatenate([acc[:, :q_lo], acc_hi], axis=1)
        oT_ref[qi] = (acc[0:v_head, :] / acc[v_head:v_head + 1, :]).astype(BF16)

    def pair(gg, carry):
        carry = group(2 * gg, 0, n_sub, *carry)
        return group(2 * gg + 1, n_sub, 0, *carry)

    m = jnp.full((1, t), -jnp.inf, F32)
    acc = jnp.zeros((vT_ref.shape[1], t), F32)
    carry = lax.fori_loop(0, qi // 2, pair, (m, mx_ref[...], acc))

    @pl.when(qi % 2 == 0)
    def _():
        finish(0, *carry)

    @pl.when(qi % 2 == 1)
    def _():
        finish(n_sub, *group(qi - 1, 0, n_sub, *carry))


def _mla_out_kernel(x_ref, oT_ref, wo_ref, o_ref):
    heads, v_head, t = oT_ref.shape
    oT = oT_ref[...].reshape(heads * v_head, t)
    y = lax.dot_general(oT, wo_ref[...], (((0,), (0,)), ((), ())), preferred_element_type=F32)
    o_ref[...] = x_ref[...] + y


def _mla_mixer(x3d, positions, gain, w_a, q_norm, w_uq, kv_norm, w_ukv, q_gain, k_gain, w_o):
    b, s, d = x3d.shape
    q_lora, kv_lora = q_norm.shape[-1], kv_norm.shape[-1]
    qk_head = q_gain.shape[-1]
    heads = w_uq.shape[1] // qk_head
    v_head = w_o.shape[0] // heads
    nope = w_ukv.shape[1] // heads - v_head
    rope = qk_head - nope
    half = rope // 2
    dk = -(-qk_head // V7X_LANES) * V7X_LANES
    dv = v_head + V7X_BF16_SUBLANES
    t, kt = ATTN_TILE, ATTN_KEY_TILE
    nt = s // t
    assert w_a.shape[1] == q_lora + kv_lora + rope and s % t == 0 and t % kt == 0

    inv_freq = 1.0 / (ROPE_THETA ** (jnp.arange(0, rope, 2, dtype=F32) / rope))
    waT = w_a.T.astype(BF16)
    wuqT = w_uq.T.astype(BF16)
    wukvT = w_ukv.T.astype(BF16)

    proj_est = (2 * t * d * 4 + (w_a.size + w_uq.size + w_ukv.size) * 2
                + 2 * heads * (2 * dk + dv) * t * 2
                + (w_a.shape[1] + w_uq.shape[1] + w_ukv.shape[1]) * t * 4 * 2)
    qT, k, vT = pl.pallas_call(
        functools.partial(_mla_proj_kernel, ts=t, heads=heads, q_lora=q_lora, kv_lora=kv_lora,
                          nope=nope, rope=rope, v_head=v_head, dk=dk, dv=dv,
                          q_scale=float(qk_head ** -0.5 * math.log2(math.e))),
        grid=(b, nt),
        in_specs=[
            pl.BlockSpec((None, t, d), lambda i, j: (i, j, 0)),
            pl.BlockSpec((None, 1, t), lambda i, j: (i, 0, j)),
            _resident((1, d)),
            _resident((half, 1)),
            _resident(waT.shape),
            _resident((q_lora, 1)),
            _resident(wuqT.shape),
            _resident((kv_lora, 1)),
            _resident(wukvT.shape),
            _resident((qk_head, 1)),
            _resident((qk_head, 1)),
        ],
        out_specs=[
            pl.BlockSpec((None, heads, None, dk, t), lambda i, j: (i, 0, j, 0, 0)),
            pl.BlockSpec((None, heads, t, dk), lambda i, j: (i, 0, j, 0)),
            pl.BlockSpec((None, heads, None, dv, t), lambda i, j: (i, 0, j, 0, 0)),
        ],
        out_shape=[
            jax.ShapeDtypeStruct((b, heads, nt, dk, t), BF16),
            jax.ShapeDtypeStruct((b, heads, s, dk), BF16),
            jax.ShapeDtypeStruct((b, heads, nt, dv, t), BF16),
        ],
        compiler_params=pltpu.CompilerParams(
            dimension_semantics=("arbitrary", "arbitrary"), vmem_limit_bytes=_vmem_limit(proj_est)),
        name="mla_proj",
    )(x3d, positions.reshape(b, 1, s), gain, inv_freq.reshape(half, 1), waT,
      q_norm.reshape(q_lora, 1), wuqT, kv_norm.reshape(kv_lora, 1), wukvT,
      q_gain.reshape(qk_head, 1), k_gain.reshape(qk_head, 1))

    attn_est = 2 * (s * dk + s * dk + dv * s + v_head * s) * 2 + 10 * kt * t * 4
    oT = pl.pallas_call(
        functools.partial(_attn_kernel, t=t, kt=kt, v_head=v_head),
        grid=(b, heads),
        in_specs=[
            pl.BlockSpec((None, None, nt, dk, t), lambda i, h: (i, h, 0, 0, 0)),
            pl.BlockSpec((None, None, s, dk), lambda i, h: (i, h, 0, 0)),
            pl.BlockSpec((None, None, nt, dv, t), lambda i, h: (i, h, 0, 0, 0)),
        ],
        out_specs=pl.BlockSpec((None, None, nt, v_head, t), lambda i, h: (i, h, 0, 0, 0)),
        out_shape=jax.ShapeDtypeStruct((b, heads, nt, v_head, t), BF16),
        scratch_shapes=[pltpu.VMEM((2 * (t // kt), kt, t), F32), pltpu.VMEM((1, t), F32)],
        compiler_params=pltpu.CompilerParams(
            dimension_semantics=("arbitrary", "arbitrary"), vmem_limit_bytes=_vmem_limit(attn_est)),
        name="mla_attn",
    )(qT, k, vT)

    out_est = 4 * t * d * 4 + 2 * heads * v_head * t * 2 + w_o.size * 2 + 2 * t * d * 4
    return pl.pallas_call(
        _mla_out_kernel,
        grid=(b, nt),
        in_specs=[
            pl.BlockSpec((None, t, d), lambda i, j: (i, j, 0)),
            pl.BlockSpec((None, heads, None, v_head, t), lambda i, j: (i, 0, j, 0, 0)),
            _resident(w_o.shape),
        ],
        out_specs=pl.BlockSpec((None, t, d), lambda i, j: (i, j, 0)),
        out_shape=jax.ShapeDtypeStruct((b, s, d), F32),
        compiler_params=pltpu.CompilerParams(
            dimension_semantics=("arbitrary", "arbitrary"), vmem_limit_bytes=_vmem_limit(out_est)),
        name="mla_out",
    )(x3d, oT, w_o.astype(BF16))


def kernel(x, positions, norm_g, ffn_gate, ffn_up, ffn_down, conv_w_in, conv_k, conv_w_out, mla_w_a, mla_q_norm, mla_w_uq, mla_kv_norm, mla_w_ukv, mla_q_gain, mla_k_gain, mla_w_o, sg_w_in, sg_v_norm, sg_w_s, sg_b, sg_w_out):
    b, s, d = x.shape
    depth = norm_g.shape[0]
    m = b * s
    assert m % TOKEN_TILE == 0 and s % TOKEN_TILE == 0

    def ffn(x3d, i, j, gi):
        out = _ffn(x3d.reshape(m, d), norm_g[i, gi].reshape(1, d), ffn_gate[i, j].astype(BF16),
                   ffn_up[i, j].astype(BF16), ffn_down[i, j].astype(BF16))
        return out.reshape(b, s, d)

    ia = ib = ic = 0
    for i in range(depth):
        x = ffn(x, i, 0, 0)
        gain = norm_g[i, 1].reshape(1, d)
        kind = i % N_MIXERS
        if kind == 0:
            x = _conv_mixer(x, gain, conv_w_in[ia].astype(BF16), conv_k[ia], conv_w_out[ia].astype(BF16))
            ia += 1
        elif kind == 1:
            x = _mla_mixer(x, positions, gain, mla_w_a[ib], mla_q_norm[ib], mla_w_uq[ib], mla_kv_norm[ib],
                           mla_w_ukv[ib], mla_q_gain[ib], mla_k_gain[ib], mla_w_o[ib])
            ib += 1
        else:
            groups, chunk = sg_w_s.shape[1], sg_w_s.shape[2]
            e = sg_w_out.shape[1]
            bias_full = jnp.repeat(sg_b[ic].T, e // groups, axis=1)
            x = _sg_mixer(x.reshape(m, d), gain, sg_w_in[ic].astype(BF16), sg_v_norm[ic].reshape(1, e),
                          sg_w_s[ic], bias_full, sg_w_out[ic].astype(BF16)).reshape(b, s, d)
            ic += 1
        x = ffn(x, i, 1, 2)
    return x
```

```python
import functools
import math

import jax
import jax.numpy as jnp
from jax import lax
from jax.experimental import pallas as pl
from jax.experimental.pallas import tpu as pltpu

EPS = 1e-6
ROPE_THETA = 10000.0
N_MIXERS = 3

F32 = jnp.float32
BF16 = jnp.bfloat16

V7X_LANES = 128
V7X_BF16_SUBLANES = 16
V7X_MXU_DIM = 256
V7X_VMEM_BYTES = 64 * 1024 * 1024

TOKEN_TILE = 512
ATTN_TILE = 512
ATTN_KEY_TILE = 256
FFN_CHUNK = 2 * V7X_MXU_DIM


def _vmem_limit(estimate_bytes):
    return int(min(estimate_bytes * 3 // 2 + (8 << 20), V7X_VMEM_BYTES - (6 << 20)))


def _resident(shape):
    nd = len(shape)
    return pl.BlockSpec(shape, lambda *_: (0,) * nd, pipeline_mode=pl.Buffered(1))


def _rms_rows(x, gain):
    ms = jnp.mean(x * x, axis=-1, keepdims=True)
    return x * lax.rsqrt(ms + EPS) * gain


def _ffn_kernel(x_ref, g_ref, wg_ref, wu_ref, wd_ref, *rest, chunks, n_cast):
    o_ref, act_ref = rest[n_cast], rest[-1]
    for src_ref, dst_ref in zip(rest[:n_cast], rest[n_cast + 1:-1]):
        dst_ref[...] = src_ref[...].astype(BF16)
    x = x_ref[...]
    hn = _rms_rows(x, g_ref[...]).astype(BF16)
    for c0, cw in chunks:
        gate = jnp.dot(hn, wg_ref[:, c0:c0 + cw], preferred_element_type=F32)
        up = jnp.dot(hn, wu_ref[:, c0:c0 + cw], preferred_element_type=F32)
        act_ref[:, c0:c0 + cw] = (gate * jax.nn.sigmoid(gate) * up).astype(BF16)
    y = jnp.dot(act_ref[...], wd_ref[...], preferred_element_type=F32)
    o_ref[...] = x + 0.5 * y


def _ffn(x2d, gain, w_gate, w_up, w_down, next_f32=None):
    m, d = x2d.shape
    f = w_gate.shape[1]
    tm = 2 * TOKEN_TILE
    steps = m // tm
    chunks = tuple((c0, min(FFN_CHUNK, f - c0)) for c0 in range(0, f, FFN_CHUNK))
    est = 4 * tm * d * 4 + 3 * d * f * 2 + tm * f * 2 + 4 * tm * FFN_CHUNK * 4 + 2 * tm * d * 4
    in_specs = [
        pl.BlockSpec((tm, d), lambda i: (i, 0)),
        _resident((1, d)),
        _resident((d, f)),
        _resident((d, f)),
        _resident((f, d)),
    ]
    out_specs = [pl.BlockSpec((tm, d), lambda i: (i, 0))]
    out_shape = [jax.ShapeDtypeStruct((m, d), F32)]
    operands = [x2d, gain, w_gate, w_up, w_down]
    n_cast = 0
    if next_f32 is not None:
        *stacked, (li, lj) = next_f32
        n_cast = len(stacked)
        for w in stacked:
            rows, cols = w.shape[2:]
            rb = next(r for r in range(V7X_BF16_SUBLANES, rows + 1, V7X_BF16_SUBLANES)
                      if rows % r == 0 and rows // r <= steps)
            last = rows // rb - 1
            in_specs.append(pl.BlockSpec(
                (None, None, rb, cols), lambda i, last=last: (li, lj, jnp.minimum(i, last), 0)))
            out_specs.append(pl.BlockSpec((rb, cols), lambda i, last=last: (jnp.minimum(i, last), 0)))
            out_shape.append(jax.ShapeDtypeStruct((rows, cols), BF16))
            operands.append(w)
            est += 2 * rb * cols * 6
    out, *cast = pl.pallas_call(
        functools.partial(_ffn_kernel, chunks=chunks, n_cast=n_cast),
        grid=(steps,),
        in_specs=in_specs,
        out_specs=out_specs,
        out_shape=out_shape,
        scratch_shapes=[pltpu.VMEM((tm, f), BF16)],
        compiler_params=pltpu.CompilerParams(
            dimension_semantics=("arbitrary",), vmem_limit_bytes=_vmem_limit(est)),
        name="ffn",
    )(*operands)
    return out, (tuple(cast) if cast else None)


def _conv_kernel(x_ref, g_ref, win_ref, ck_ref, wout_ref, o_ref, z_ref, *, ts, d, halo):
    @pl.when(pl.program_id(1) == 0)
    def _():
        z_ref[0:halo, :] = jnp.zeros((halo, d), F32)

    x = x_ref[...]
    hn = _rms_rows(x, g_ref[...]).astype(BF16)
    b_gate = jnp.dot(hn, win_ref[:, 0:d], preferred_element_type=F32)
    c_gate = jnp.dot(hn, win_ref[:, d:2 * d], preferred_element_type=F32)
    h = jnp.dot(hn, win_ref[:, 2 * d:3 * d], preferred_element_type=F32)
    z = c_gate * h
    z_ref[halo:halo + ts, :] = z
    ck = ck_ref[...]
    conv = (ck[0:1, :] * z_ref[halo - 2:halo - 2 + ts, :]
            + ck[1:2, :] * z_ref[halo - 1:halo - 1 + ts, :]
            + ck[2:3, :] * z)
    z_ref[0:halo, :] = z[ts - halo:ts, :]
    y = jnp.dot((b_gate * conv).astype(BF16), wout_ref[...], preferred_element_type=F32)
    o_ref[...] = x + y


def _conv_mixer(x3d, gain, w_in, conv_k, w_out):
    b, s, d = x3d.shape
    ts = TOKEN_TILE
    halo = 8
    assert conv_k.shape[0] == 3 and halo >= conv_k.shape[0] - 1
    est = 4 * ts * d * 4 + 4 * d * d * 2 + (ts + halo) * d * 4 + 6 * ts * d * 4
    return pl.pallas_call(
        functools.partial(_conv_kernel, ts=ts, d=d, halo=halo),
        grid=(b, s // ts),
        in_specs=[
            pl.BlockSpec((None, ts, d), lambda i, j: (i, j, 0)),
            _resident((1, d)),
            _resident((d, 3 * d)),
            _resident((3, d)),
            _resident((d, d)),
        ],
        out_specs=pl.BlockSpec((None, ts, d), lambda i, j: (i, j, 0)),
        out_shape=jax.ShapeDtypeStruct((b, s, d), F32),
        scratch_shapes=[pltpu.VMEM((ts + halo, d), F32)],
        compiler_params=pltpu.CompilerParams(
            dimension_semantics=("arbitrary", "arbitrary"), vmem_limit_bytes=_vmem_limit(est)),
        name="conv_mixer",
    )(x3d, gain, w_in, conv_k, w_out)


def _sg_kernel(x_ref, g_ref, win_ref, vn_ref, ws_ref, bias_ref, wout_ref, o_ref, gated_ref,
               *, ts, e, groups, chunk):
    x = x_ref[...]
    hn = _rms_rows(x, g_ref[...]).astype(BF16)
    u = jax.nn.gelu(jnp.dot(hn, win_ref[:, 0:e], preferred_element_type=F32))
    v = jax.nn.gelu(jnp.dot(hn, win_ref[:, e:2 * e], preferred_element_type=F32))
    v = _rms_rows(v, vn_ref[...]).astype(BF16)
    cg = e // groups
    row = lax.broadcasted_iota(jnp.int32, (chunk, chunk), 0)
    col = lax.broadcasted_iota(jnp.int32, (chunk, chunk), 1)
    causal = col <= row
    for gi in range(groups):
        w_masked = jnp.where(causal, ws_ref[gi], 0.0).astype(BF16)
        for ci in range(ts // chunk):
            rows = slice(ci * chunk, (ci + 1) * chunk)
            cols = slice(gi * cg, (gi + 1) * cg)
            mixed = jnp.dot(w_masked, v[rows, cols], preferred_element_type=F32) + bias_ref[:, cols]
            gated_ref[rows, cols] = (u[rows, cols] * mixed).astype(BF16)
    y = jnp.dot(gated_ref[...], wout_ref[...], preferred_element_type=F32)
    o_ref[...] = x + y


def _sg_mixer(x2d, gain, w_in, v_norm, w_s, bias_full, w_out):
    m, d = x2d.shape
    e = w_out.shape[0]
    groups, chunk, _ = w_s.shape
    ts = TOKEN_TILE
    assert ts % chunk == 0 and (e // groups) % V7X_LANES == 0
    est = 4 * ts * d * 4 + (d * 2 * e + e * d) * 2 + groups * chunk * chunk * 4 + 6 * ts * e * 4
    return pl.pallas_call(
        functools.partial(_sg_kernel, ts=ts, e=e, groups=groups, chunk=chunk),
        grid=(m // ts,),
        in_specs=[
            pl.BlockSpec((ts, d), lambda i: (i, 0)),
            _resident((1, d)),
            _resident((d, 2 * e)),
            _resident((1, e)),
            _resident((groups, chunk, chunk)),
            _resident((chunk, e)),
            _resident((e, d)),
        ],
        out_specs=pl.BlockSpec((ts, d), lambda i: (i, 0)),
        out_shape=jax.ShapeDtypeStruct((m, d), F32),
        scratch_shapes=[pltpu.VMEM((ts, e), BF16)],
        compiler_params=pltpu.CompilerParams(
            dimension_semantics=("arbitrary",), vmem_limit_bytes=_vmem_limit(est)),
        name="sg_mixer",
    )(x2d, gain, w_in, v_norm, w_s, bias_full, w_out)


def _mla_proj_kernel(x_ref, pos_ref, g_ref, inv_freq_ref, waT_ref, qn_ref, wuqT_ref, kvn_ref, wukvT_ref,
                     qg_ref, kg_ref, qT_ref, k_ref, vT_ref,
                     *, ts, heads, q_lora, kv_lora, nope, rope, v_head, dk, dv, q_scale):
    half = rope // 2
    qk_head = nope + rope
    hn = _rms_rows(x_ref[...], g_ref[...]).astype(BF16)
    aT = lax.dot_general(waT_ref[...], hn, (((1,), (1,)), ((), ())), preferred_element_type=F32)
    c_q = aT[0:q_lora, :]
    c_kv = aT[q_lora:q_lora + kv_lora, :]
    k_pe = aT[q_lora + kv_lora:q_lora + kv_lora + rope, :]

    def rms_cols(t, gain_col):
        ms = jnp.mean(t * t, axis=0, keepdims=True)
        return t * lax.rsqrt(ms + EPS) * gain_col

    cqn = rms_cols(c_q, qn_ref[...]).astype(BF16)
    ckvn = rms_cols(c_kv, kvn_ref[...]).astype(BF16)
    qT = jnp.dot(wuqT_ref[...], cqn, preferred_element_type=F32)
    kvT = jnp.dot(wukvT_ref[...], ckvn, preferred_element_type=F32)

    ang = inv_freq_ref[...] * pos_ref[...].astype(F32)
    cos = jnp.cos(ang)
    sin = jnp.sin(ang)

    def rotate(t):
        t1, t2 = t[0:half, :], t[half:rope, :]
        return t1 * cos - t2 * sin, t2 * cos + t1 * sin

    q_gain = qg_ref[...] * q_scale
    k_gain = kg_ref[...]
    pe_sq = jnp.sum(k_pe * k_pe, axis=0, keepdims=True)
    zeros_q = jnp.zeros((dk - qk_head, ts), BF16)
    ones_row = (lax.broadcasted_iota(jnp.int32, (dv - v_head, ts), 0) == 0).astype(BF16)
    for h in range(heads):
        qh = qT[h * qk_head:(h + 1) * qk_head, :]
        qh = rms_cols(qh, q_gain)
        r1, r2 = rotate(qh[nope:qk_head, :])
        qT_ref[h, 0:nope, :] = qh[0:nope, :].astype(BF16)
        qT_ref[h, nope:nope + half, :] = r1.astype(BF16)
        qT_ref[h, nope + half:qk_head, :] = r2.astype(BF16)
        qT_ref[h, qk_head:dk, :] = zeros_q

        base = h * (nope + v_head)
        kn = kvT[base:base + nope, :]
        ms = (jnp.sum(kn * kn, axis=0, keepdims=True) + pe_sq) * (1.0 / qk_head)
        rs = lax.rsqrt(ms + EPS)
        kn = kn * rs * k_gain[0:nope, :]
        r1, r2 = rotate(k_pe * rs * k_gain[nope:qk_head, :])
        k_fm = jnp.concatenate([kn, r1, r2, jnp.zeros((dk - qk_head, ts), F32)], axis=0)
        k_ref[h] = k_fm.T.astype(BF16)

        vT_ref[h, 0:v_head, :] = kvT[base + nope:base + nope + v_head, :].astype(BF16)
        vT_ref[h, v_head:dv, :] = ones_row


def _attn_kernel(qT_ref, k_ref, vT_ref, oT_ref, s_ref, mx_ref, *, t, kt, v_head):
    n_sub = t // kt
    n_tiles = qT_ref.shape[0]

    def produce(qT, g, wr):
        mx = None
        for n in range(n_sub):
            start = (g * n_sub + n) * kt
            if not isinstance(start, int):
                start = pl.multiple_of(start, kt)
            sc = jnp.dot(k_ref[pl.ds(start, kt), :], qT, preferred_element_type=F32)
            s_ref[wr + n] = sc
            sub_max = jnp.max(sc, axis=0, keepdims=True)
            mx = sub_max if mx is None else jnp.maximum(mx, sub_max)
        return mx

    mx_ref[...] = produce(qT_ref[0], 0, 0)

    def q_tile(qi, _):
        _attn_query_tile(qi, qT_ref, vT_ref, oT_ref, s_ref, mx_ref, produce,
                         t=t, kt=kt, v_head=v_head, n_sub=n_sub, n_tiles=n_tiles)
        return 0

    lax.fori_loop(0, n_tiles, q_tile, 0)


def _attn_query_tile(qi, qT_ref, vT_ref, oT_ref, s_ref, mx_ref, produce, *, t, kt, v_head, n_sub, n_tiles):
    qT = qT_ref[qi]

    def group(g, rd, wr, m, mx, acc):
        mx_next = produce(qT, g + 1, wr)
        m_new = jnp.maximum(m, mx)
        alpha = jnp.exp2(m - m_new)
        p = jnp.concatenate([jnp.exp2(s_ref[rd + n] - m_new).astype(BF16) for n in range(n_sub)], axis=0)
        acc = alpha * acc + jnp.dot(vT_ref[g], p, preferred_element_type=F32)
        return m_new, mx_next, acc

    def finish(rd, m, mx, acc):
        del mx
        masked = []
        for n in range(n_sub):
            s = s_ref[rd + n, :, n * kt:]
            k_pos = lax.broadcasted_iota(jnp.int32, s.shape, 0)
            q_pos = lax.broadcasted_iota(jnp.int32, s.shape, 1)
            masked.append(jnp.where(k_pos <= q_pos, s, -jnp.inf))
        vT = vT_ref[qi]
        mx_ref[...] = produce(qT_ref[jnp.minimum(qi + 1, n_tiles - 1)], 0, 0)
        for n in range(n_sub):
            q_lo = n * kt
            s, m_old, acc_old = masked[n], m[:, q_lo:], acc[:, q_lo:]
            m_hi = jnp.maximum(m_old, jnp.max(s, axis=0, keepdims=True))
            p = jnp.exp2(s - m_hi).astype(BF16)
            acc_hi = jnp.exp2(m_old - m_hi) * acc_old + jnp.dot(
                vT[:, q_lo:q_lo + kt], p, preferred_element_type=F32)
            if n == 0:
                m, acc = m_hi, acc_hi
            else:
                m = jnp.concatenate([m[:, :q_lo], m_hi], axis=1)
                acc = jnp.concatenate([acc[:, :q_lo], acc_hi], axis=1)
        oT_ref[qi] = (acc[0:v_head, :] / acc[v_head:v_head + 1, :]).astype(BF16)

    def pair(gg, carry):
        carry = group(2 * gg, 0, n_sub, *carry)
        return group(2 * gg + 1, n_sub, 0, *carry)

    m = jnp.full((1, t), -jnp.inf, F32)
    acc = jnp.zeros((vT_ref.shape[1], t), F32)
    carry = lax.fori_loop(0, qi // 2, pair, (m, mx_ref[...], acc))

    @pl.when(qi % 2 == 0)
    def _():
        finish(0, *carry)

    @pl.when(qi % 2 == 1)
    def _():
        finish(n_sub, *group(qi - 1, 0, n_sub, *carry))


def _mla_out_kernel(x_ref, oT_ref, wo_ref, o_ref):
    heads, v_head, t = oT_ref.shape
    oT = oT_ref[...].reshape(heads * v_head, t)
    y = lax.dot_general(oT, wo_ref[...], (((0,), (0,)), ((), ())), preferred_element_type=F32)
    o_ref[...] = x_ref[...] + y


def _mla_mixer(x3d, positions, gain, w_a, q_norm, w_uq, kv_norm, w_ukv, q_gain, k_gain, w_o):
    b, s, d = x3d.shape
    q_lora, kv_lora = q_norm.shape[-1], kv_norm.shape[-1]
    qk_head = q_gain.shape[-1]
    heads = w_uq.shape[1] // qk_head
    v_head = w_o.shape[0] // heads
    nope = w_ukv.shape[1] // heads - v_head
    rope = qk_head - nope
    half = rope // 2
    dk = -(-qk_head // V7X_LANES) * V7X_LANES
    dv = v_head + V7X_BF16_SUBLANES
    t, kt = ATTN_TILE, ATTN_KEY_TILE
    nt = s // t
    assert w_a.shape[1] == q_lora + kv_lora + rope and s % t == 0 and t % kt == 0

    inv_freq = 1.0 / (ROPE_THETA ** (jnp.arange(0, rope, 2, dtype=F32) / rope))
    waT = w_a.T.astype(BF16)
    wuqT = w_uq.T.astype(BF16)
    wukvT = w_ukv.T.astype(BF16)

    proj_est = (2 * t * d * 4 + (w_a.size + w_uq.size + w_ukv.size) * 2
                + 2 * heads * (2 * dk + dv) * t * 2
                + (w_a.shape[1] + w_uq.shape[1] + w_ukv.shape[1]) * t * 4 * 2)
    qT, k, vT = pl.pallas_call(
        functools.partial(_mla_proj_kernel, ts=t, heads=heads, q_lora=q_lora, kv_lora=kv_lora,
                          nope=nope, rope=rope, v_head=v_head, dk=dk, dv=dv,
                          q_scale=float(qk_head ** -0.5 * math.log2(math.e))),
        grid=(b, nt),
        in_specs=[
            pl.BlockSpec((None, t, d), lambda i, j: (i, j, 0)),
            pl.BlockSpec((None, 1, t), lambda i, j: (i, 0, j)),
            _resident((1, d)),
            _resident((half, 1)),
            _resident(waT.shape),
            _resident((q_lora, 1)),
            _resident(wuqT.shape),
            _resident((kv_lora, 1)),
            _resident(wukvT.shape),
            _resident((qk_head, 1)),
            _resident((qk_head, 1)),
        ],
        out_specs=[
            pl.BlockSpec((None, heads, None, dk, t), lambda i, j: (i, 0, j, 0, 0)),
            pl.BlockSpec((None, heads, t, dk), lambda i, j: (i, 0, j, 0)),
            pl.BlockSpec((None, heads, None, dv, t), lambda i, j: (i, 0, j, 0, 0)),
        ],
        out_shape=[
            jax.ShapeDtypeStruct((b, heads, nt, dk, t), BF16),
            jax.ShapeDtypeStruct((b, heads, s, dk), BF16),
            jax.ShapeDtypeStruct((b, heads, nt, dv, t), BF16),
        ],
        compiler_params=pltpu.CompilerParams(
            dimension_semantics=("arbitrary", "arbitrary"), vmem_limit_bytes=_vmem_limit(proj_est)),
        name="mla_proj",
    )(x3d, positions.reshape(b, 1, s), gain, inv_freq.reshape(half, 1), waT,
      q_norm.reshape(q_lora, 1), wuqT, kv_norm.reshape(kv_lora, 1), wukvT,
      q_gain.reshape(qk_head, 1), k_gain.reshape(qk_head, 1))

    attn_est = 2 * (s * dk + s * dk + dv * s + v_head * s) * 2 + 10 * kt * t * 4
    oT = pl.pallas_call(
        functools.partial(_attn_kernel, t=t, kt=kt, v_head=v_head),
        grid=(b, heads),
        in_specs=[
            pl.BlockSpec((None, None, nt, dk, t), lambda i, h: (i, h, 0, 0, 0)),
            pl.BlockSpec((None, None, s, dk), lambda i, h: (i, h, 0, 0)),
            pl.BlockSpec((None, None, nt, dv, t), lambda i, h: (i, h, 0, 0, 0)),
        ],
        out_specs=pl.BlockSpec((None, None, nt, v_head, t), lambda i, h: (i, h, 0, 0, 0)),
        out_shape=jax.ShapeDtypeStruct((b, heads, nt, v_head, t), BF16),
        scratch_shapes=[pltpu.VMEM((2 * (t // kt), kt, t), F32), pltpu.VMEM((1, t), F32)],
        compiler_params=pltpu.CompilerParams(
            dimension_semantics=("arbitrary", "arbitrary"), vmem_limit_bytes=_vmem_limit(attn_est)),
        name="mla_attn",
    )(qT, k, vT)

    out_est = 4 * t * d * 4 + 2 * heads * v_head * t * 2 + w_o.size * 2 + 2 * t * d * 4
    return pl.pallas_call(
        _mla_out_kernel,
        grid=(b, nt),
        in_specs=[
            pl.BlockSpec((None, t, d), lambda i, j: (i, j, 0)),
            pl.BlockSpec((None, heads, None, v_head, t), lambda i, j: (i, 0, j, 0, 0)),
            _resident(w_o.shape),
        ],
        out_specs=pl.BlockSpec((None, t, d), lambda i, j: (i, j, 0)),
        out_shape=jax.ShapeDtypeStruct((b, s, d), F32),
        compiler_params=pltpu.CompilerParams(
            dimension_semantics=("arbitrary", "arbitrary"), vmem_limit_bytes=_vmem_limit(out_est)),
        name="mla_out",
    )(x3d, oT, w_o.astype(BF16))


def kernel(x, positions, norm_g, ffn_gate, ffn_up, ffn_down, conv_w_in, conv_k, conv_w_out, mla_w_a, mla_q_norm, mla_w_uq, mla_kv_norm, mla_w_ukv, mla_q_gain, mla_k_gain, mla_w_o, sg_w_in, sg_v_norm, sg_w_s, sg_b, sg_w_out):
    b, s, d = x.shape
    depth = norm_g.shape[0]
    m = b * s
    assert m % TOKEN_TILE == 0 and s % TOKEN_TILE == 0

    ffn_w = [(ffn_gate[0, 0].astype(BF16), ffn_up[0, 0].astype(BF16), ffn_down[0, 0].astype(BF16))]

    def ffn(x3d, i, j, gi):
        nxt = (i, j + 1) if j == 0 else (i + 1, 0)
        next_f32 = (ffn_gate, ffn_up, ffn_down, nxt) if nxt[0] < depth else None
        out, cast = _ffn(x3d.reshape(m, d), norm_g[i, gi].reshape(1, d), *ffn_w[0], next_f32=next_f32)
        ffn_w[0] = cast
        return out.reshape(b, s, d)

    ia = ib = ic = 0
    for i in range(depth):
        x = ffn(x, i, 0, 0)
        gain = norm_g[i, 1].reshape(1, d)
        kind = i % N_MIXERS
        if kind == 0:
            x = _conv_mixer(x, gain, conv_w_in[ia].astype(BF16), conv_k[ia], conv_w_out[ia].astype(BF16))
            ia += 1
        elif kind == 1:
            x = _mla_mixer(x, positions, gain, mla_w_a[ib], mla_q_norm[ib], mla_w_uq[ib], mla_kv_norm[ib],
                           mla_w_ukv[ib], mla_q_gain[ib], mla_k_gain[ib], mla_w_o[ib])
            ib += 1
        else:
            groups, chunk = sg_w_s.shape[1], sg_w_s.shape[2]
            e = sg_w_out.shape[1]
            bias_full = jnp.repeat(sg_b[ic].T, e // groups, axis=1)
            x = _sg_mixer(x.reshape(m, d), gain, sg_w_in[ic].astype(BF16), sg_v_norm[ic].reshape(1, e),
                          sg_w_s[ic], bias_full, sg_w_out[ic].astype(BF16)).reshape(b, s, d)
            ic += 1
        x = ffn(x, i, 1, 2)
    return x
```

```python
import functools
import math

import jax
import jax.numpy as jnp
from jax import lax
from jax.experimental import pallas as pl
from jax.experimental.pallas import tpu as pltpu

EPS = 1e-6
ROPE_THETA = 10000.0
N_MIXERS = 3

F32 = jnp.float32
BF16 = jnp.bfloat16

V7X_LANES = 128
V7X_BF16_SUBLANES = 16
V7X_MXU_DIM = 256
V7X_VMEM_BYTES = 64 * 1024 * 1024

TOKEN_TILE = 512
ATTN_TILE = 512
ATTN_KEY_TILE = 256
FFN_CHUNK = 2 * V7X_MXU_DIM


def _vmem_limit(estimate_bytes):
    return int(min(estimate_bytes * 3 // 2 + (8 << 20), V7X_VMEM_BYTES - (6 << 20)))


def _resident(shape):
    nd = len(shape)
    return pl.BlockSpec(shape, lambda *_: (0,) * nd, pipeline_mode=pl.Buffered(1))


def _rms_rows(x, gain):
    ms = jnp.mean(x * x, axis=-1, keepdims=True)
    return x * lax.rsqrt(ms + EPS) * gain


def _ffn_kernel(x_ref, g_ref, wg_ref, wu_ref, wd_ref, *rest, chunks, n_cast):
    o_ref, act_ref = rest[n_cast], rest[-1]
    for src_ref, dst_ref in zip(rest[:n_cast], rest[n_cast + 1:-1]):
        dst_ref[...] = src_ref[...].astype(BF16)
    x = x_ref[...]
    hn = _rms_rows(x, g_ref[...]).astype(BF16)
    for c0, cw in chunks:
        gate = jnp.dot(hn, wg_ref[:, c0:c0 + cw], preferred_element_type=F32)
        up = jnp.dot(hn, wu_ref[:, c0:c0 + cw], preferred_element_type=F32)
        act_ref[:, c0:c0 + cw] = (gate * jax.nn.sigmoid(gate) * up).astype(BF16)
    y = jnp.dot(act_ref[...], wd_ref[...], preferred_element_type=F32)
    o_ref[...] = x + 0.5 * y


def _ffn(x2d, gain, w_gate, w_up, w_down, next_f32=None):
    m, d = x2d.shape
    f = w_gate.shape[1]
    tm = 2 * TOKEN_TILE
    steps = m // tm
    chunks = tuple((c0, min(FFN_CHUNK, f - c0)) for c0 in range(0, f, FFN_CHUNK))
    est = 4 * tm * d * 4 + 3 * d * f * 2 + tm * f * 2 + 4 * tm * FFN_CHUNK * 4 + 2 * tm * d * 4
    in_specs = [
        pl.BlockSpec((tm, d), lambda i: (i, 0)),
        _resident((1, d)),
        _resident((d, f)),
        _resident((d, f)),
        _resident((f, d)),
    ]
    out_specs = [pl.BlockSpec((tm, d), lambda i: (i, 0))]
    out_shape = [jax.ShapeDtypeStruct((m, d), F32)]
    operands = [x2d, gain, w_gate, w_up, w_down]
    n_cast = 0
    if next_f32 is not None:
        *stacked, (li, lj) = next_f32
        n_cast = len(stacked)
        for w in stacked:
            rows, cols = w.shape[2:]
            rb = next(r for r in range(V7X_BF16_SUBLANES, rows + 1, V7X_BF16_SUBLANES)
                      if rows % r == 0 and rows // r <= steps)
            last = rows // rb - 1
            in_specs.append(pl.BlockSpec(
                (None, None, rb, cols), lambda i, last=last: (li, lj, jnp.minimum(i, last), 0)))
            out_specs.append(pl.BlockSpec((rb, cols), lambda i, last=last: (jnp.minimum(i, last), 0)))
            out_shape.append(jax.ShapeDtypeStruct((rows, cols), BF16))
            operands.append(w)
            est += 2 * rb * cols * 6
    out, *cast = pl.pallas_call(
        functools.partial(_ffn_kernel, chunks=chunks, n_cast=n_cast),
        grid=(steps,),
        in_specs=in_specs,
        out_specs=out_specs,
        out_shape=out_shape,
        scratch_shapes=[pltpu.VMEM((tm, f), BF16)],
        compiler_params=pltpu.CompilerParams(
            dimension_semantics=("arbitrary",), vmem_limit_bytes=_vmem_limit(est)),
        name="ffn",
    )(*operands)
    return out, (tuple(cast) if cast else None)


def _conv_kernel(x_ref, g_ref, win_ref, ck_ref, wout_ref, o_ref, z_ref, *, ts, d, halo, n_sub):
    @pl.when(pl.program_id(1) == 0)
    def _():
        z_ref[0:halo, :] = jnp.zeros((halo, d), F32)

    sub = ts // n_sub
    b_gates = []
    for i in range(n_sub):
        hn = _rms_rows(x_ref[i * sub:(i + 1) * sub, :], g_ref[...]).astype(BF16)
        b_gates.append(jnp.dot(hn, win_ref[:, 0:d], preferred_element_type=F32))
        c_gate = jnp.dot(hn, win_ref[:, d:2 * d], preferred_element_type=F32)
        h = jnp.dot(hn, win_ref[:, 2 * d:3 * d], preferred_element_type=F32)
        z_ref[halo + i * sub:halo + (i + 1) * sub, :] = c_gate * h
    ck = ck_ref[...]
    for i in range(n_sub):
        r0 = halo + i * sub
        conv = (ck[0:1, :] * z_ref[r0 - 2:r0 - 2 + sub, :]
                + ck[1:2, :] * z_ref[r0 - 1:r0 - 1 + sub, :]
                + ck[2:3, :] * z_ref[r0:r0 + sub, :])
        y = jnp.dot((b_gates[i] * conv).astype(BF16), wout_ref[...], preferred_element_type=F32)
        o_ref[i * sub:(i + 1) * sub, :] = x_ref[i * sub:(i + 1) * sub, :] + y
    z_ref[0:halo, :] = z_ref[ts:ts + halo, :]


def _conv_mixer(x3d, gain, w_in, conv_k, w_out):
    b, s, d = x3d.shape
    n_sub = 2
    ts = n_sub * TOKEN_TILE
    halo = 8
    assert conv_k.shape[0] == 3 and halo >= conv_k.shape[0] - 1
    est = 4 * ts * d * 4 + 4 * d * d * 2 + (ts + halo) * d * 4 + 5 * ts * d * 4
    return pl.pallas_call(
        functools.partial(_conv_kernel, ts=ts, d=d, halo=halo, n_sub=n_sub),
        grid=(b, s // ts),
        in_specs=[
            pl.BlockSpec((None, ts, d), lambda i, j: (i, j, 0)),
            _resident((1, d)),
            _resident((d, 3 * d)),
            _resident((3, d)),
            _resident((d, d)),
        ],
        out_specs=pl.BlockSpec((None, ts, d), lambda i, j: (i, j, 0)),
        out_shape=jax.ShapeDtypeStruct((b, s, d), F32),
        scratch_shapes=[pltpu.VMEM((ts + halo, d), F32)],
        compiler_params=pltpu.CompilerParams(
            dimension_semantics=("arbitrary", "arbitrary"), vmem_limit_bytes=_vmem_limit(est)),
        name="conv_mixer",
    )(x3d, gain, w_in, conv_k, w_out)


def _sg_kernel(x_ref, g_ref, win_ref, vn_ref, ws_ref, bias_ref, wout_ref, o_ref, gated_ref,
               *, ts, e, groups, chunk, n_sub):
    sub = ts // n_sub
    us, vs = [], []
    for i in range(n_sub):
        hn = _rms_rows(x_ref[i * sub:(i + 1) * sub, :], g_ref[...]).astype(BF16)
        us.append(jax.nn.gelu(jnp.dot(hn, win_ref[:, 0:e], preferred_element_type=F32)))
        v = jax.nn.gelu(jnp.dot(hn, win_ref[:, e:2 * e], preferred_element_type=F32))
        vs.append(_rms_rows(v, vn_ref[...]).astype(BF16))
    cg = e // groups
    row = lax.broadcasted_iota(jnp.int32, (chunk, chunk), 0)
    col = lax.broadcasted_iota(jnp.int32, (chunk, chunk), 1)
    causal = col <= row
    w_masked = [jnp.where(causal, ws_ref[gi], 0.0).astype(BF16) for gi in range(groups)]
    for i in range(n_sub):
        u, v = us[i], vs[i]
        for gi in range(groups):
            for ci in range(sub // chunk):
                rows = slice(ci * chunk, (ci + 1) * chunk)
                cols = slice(gi * cg, (gi + 1) * cg)
                mixed = jnp.dot(w_masked[gi], v[rows, cols], preferred_element_type=F32) + bias_ref[:, cols]
                gated_ref[i * sub + ci * chunk:i * sub + (ci + 1) * chunk, cols] = (u[rows, cols] * mixed).astype(BF16)
        y = jnp.dot(gated_ref[i * sub:(i + 1) * sub, :], wout_ref[...], preferred_element_type=F32)
        o_ref[i * sub:(i + 1) * sub, :] = x_ref[i * sub:(i + 1) * sub, :] + y


def _sg_mixer(x2d, gain, w_in, v_norm, w_s, bias_full, w_out):
    m, d = x2d.shape
    e = w_out.shape[0]
    groups, chunk, _ = w_s.shape
    n_sub = 2
    ts = n_sub * TOKEN_TILE
    assert TOKEN_TILE % chunk == 0 and (e // groups) % V7X_LANES == 0
    est = 4 * ts * d * 4 + (d * 2 * e + e * d) * 2 + groups * chunk * chunk * 4 + 4 * ts * e * 4
    return pl.pallas_call(
        functools.partial(_sg_kernel, ts=ts, e=e, groups=groups, chunk=chunk, n_sub=n_sub),
        grid=(m // ts,),
        in_specs=[
            pl.BlockSpec((ts, d), lambda i: (i, 0)),
            _resident((1, d)),
            _resident((d, 2 * e)),
            _resident((1, e)),
            _resident((groups, chunk, chunk)),
            _resident((chunk, e)),
            _resident((e, d)),
        ],
        out_specs=pl.BlockSpec((ts, d), lambda i: (i, 0)),
        out_shape=jax.ShapeDtypeStruct((m, d), F32),
        scratch_shapes=[pltpu.VMEM((ts, e), BF16)],
        compiler_params=pltpu.CompilerParams(
            dimension_semantics=("arbitrary",), vmem_limit_bytes=_vmem_limit(est)),
        name="sg_mixer",
    )(x2d, gain, w_in, v_norm, w_s, bias_full, w_out)


def _mla_proj_kernel(x_ref, pos_ref, g_ref, inv_freq_ref, waT_ref, qn_ref, wuqT_ref, kvn_ref, wukvT_ref,
                     qg_ref, kg_ref, qT_ref, k_ref, vT_ref,
                     *, ts, heads, q_lora, kv_lora, nope, rope, v_head, dk, dv, q_scale):
    half = rope // 2
    qk_head = nope + rope
    hn = _rms_rows(x_ref[...], g_ref[...]).astype(BF16)
    aT = lax.dot_general(waT_ref[...], hn, (((1,), (1,)), ((), ())), preferred_element_type=F32)
    c_q = aT[0:q_lora, :]
    c_kv = aT[q_lora:q_lora + kv_lora, :]
    k_pe = aT[q_lora + kv_lora:q_lora + kv_lora + rope, :]

    def rms_cols(t, gain_col):
        ms = jnp.mean(t * t, axis=0, keepdims=True)
        return t * lax.rsqrt(ms + EPS) * gain_col

    cqn = rms_cols(c_q, qn_ref[...]).astype(BF16)
    ckvn = rms_cols(c_kv, kvn_ref[...]).astype(BF16)
    qT = jnp.dot(wuqT_ref[...], cqn, preferred_element_type=F32)
    kvT = jnp.dot(wukvT_ref[...], ckvn, preferred_element_type=F32)

    ang = inv_freq_ref[...] * pos_ref[...].astype(F32)
    cos = jnp.cos(ang)
    sin = jnp.sin(ang)

    def rotate(t):
        t1, t2 = t[0:half, :], t[half:rope, :]
        return t1 * cos - t2 * sin, t2 * cos + t1 * sin

    q_gain = qg_ref[...] * q_scale
    k_gain = kg_ref[...]
    pe_sq = jnp.sum(k_pe * k_pe, axis=0, keepdims=True)
    zeros_q = jnp.zeros((dk - qk_head, ts), BF16)
    ones_row = (lax.broadcasted_iota(jnp.int32, (dv - v_head, ts), 0) == 0).astype(BF16)
    for h in range(heads):
        qh = qT[h * qk_head:(h + 1) * qk_head, :]
        qh = rms_cols(qh, q_gain)
        r1, r2 = rotate(qh[nope:qk_head, :])
        qT_ref[h, 0:nope, :] = qh[0:nope, :].astype(BF16)
        qT_ref[h, nope:nope + half, :] = r1.astype(BF16)
        qT_ref[h, nope + half:qk_head, :] = r2.astype(BF16)
        qT_ref[h, qk_head:dk, :] = zeros_q

        base = h * (nope + v_head)
        kn = kvT[base:base + nope, :]
        ms = (jnp.sum(kn * kn, axis=0, keepdims=True) + pe_sq) * (1.0 / qk_head)
        rs = lax.rsqrt(ms + EPS)
        kn = kn * rs * k_gain[0:nope, :]
        r1, r2 = rotate(k_pe * rs * k_gain[nope:qk_head, :])
        k_fm = jnp.concatenate([kn, r1, r2, jnp.zeros((dk - qk_head, ts), F32)], axis=0)
        k_ref[h] = k_fm.T.astype(BF16)

        vT_ref[h, 0:v_head, :] = kvT[base + nope:base + nope + v_head, :].astype(BF16)
        vT_ref[h, v_head:dv, :] = ones_row


def _attn_kernel(qT_ref, k_ref, vT_ref, oT_ref, s_ref, mx_ref, *, t, kt, v_head):
    n_sub = t // kt
    n_tiles = qT_ref.shape[0]

    def produce(qT, g, wr):
        mx = None
        for n in range(n_sub):
            start = (g * n_sub + n) * kt
            if not isinstance(start, int):
                start = pl.multiple_of(start, kt)
            sc = jnp.dot(k_ref[pl.ds(start, kt), :], qT, preferred_element_type=F32)
            s_ref[wr + n, :, 0:t] = sc
            sub_max = jnp.max(sc, axis=0, keepdims=True)
            mx = sub_max if mx is None else jnp.maximum(mx, sub_max)
        return mx

    mx_ref[...] = produce(qT_ref[0], 0, 0)

    def q_tile(qi, _):
        _attn_query_tile(qi, qT_ref, vT_ref, oT_ref, s_ref, mx_ref, produce,
                         t=t, kt=kt, v_head=v_head, n_sub=n_sub, n_tiles=n_tiles)
        return 0

    lax.fori_loop(0, n_tiles, q_tile, 0)


def _attn_query_tile(qi, qT_ref, vT_ref, oT_ref, s_ref, mx_ref, produce, *, t, kt, v_head, n_sub, n_tiles):
    qT = qT_ref[qi]

    def group(g, rd, wr, m, mx, acc):
        mx_next = produce(qT, g + 1, wr)
        m_new = jnp.maximum(m, mx)
        alpha = jnp.exp2(m - m_new)
        p = jnp.concatenate([jnp.exp2(s_ref[rd + n, :, 0:t] - m_new).astype(BF16) for n in range(n_sub)], axis=0)
        acc = alpha * acc + jnp.dot(vT_ref[g], p, preferred_element_type=F32)
        return m_new, mx_next, acc

    def finish(rd, m, mx, acc):
        del mx
        masked = []
        for n in range(n_sub):
            s = s_ref[rd + n, :, n * kt:t]
            k_pos = lax.broadcasted_iota(jnp.int32, s.shape, 0)
            q_pos = lax.broadcasted_iota(jnp.int32, s.shape, 1)
            masked.append(jnp.where(k_pos <= q_pos, s, -jnp.inf))
        vT = vT_ref[qi]
        mx_ref[...] = produce(qT_ref[jnp.minimum(qi + 1, n_tiles - 1)], 0, 0)
        for n in range(n_sub):
            q_lo = n * kt
            s, m_old, acc_old = masked[n], m[:, q_lo:], acc[:, q_lo:]
            m_hi = jnp.maximum(m_old, jnp.max(s, axis=0, keepdims=True))
            p = jnp.exp2(s - m_hi).astype(BF16)
            acc_hi = jnp.exp2(m_old - m_hi) * acc_old + jnp.dot(
                vT[:, q_lo:q_lo + kt], p, preferred_element_type=F32)
            if n == 0:
                m, acc = m_hi, acc_hi
            else:
                m = jnp.concatenate([m[:, :q_lo], m_hi], axis=1)
                acc = jnp.concatenate([acc[:, :q_lo], acc_hi], axis=1)
        oT_ref[qi] = (acc[0:v_head, :] / acc[v_head:v_head + 1, :]).astype(BF16)

    def pair(gg, carry):
        carry = group(2 * gg, 0, n_sub, *carry)
        return group(2 * gg + 1, n_sub, 0, *carry)

    m = jnp.full((1, t), -jnp.inf, F32)
    acc = jnp.zeros((vT_ref.shape[1], t), F32)
    carry = lax.fori_loop(0, qi // 2, pair, (m, mx_ref[...], acc))

    @pl.when(qi % 2 == 0)
    def _():
        finish(0, *carry)

    @pl.when(qi % 2 == 1)
    def _():
        finish(n_sub, *group(qi - 1, 0, n_sub, *carry))


def _mla_out_kernel(x_ref, oT_ref, wo_ref, o_ref):
    heads, v_head, t = oT_ref.shape
    oT = oT_ref[...].reshape(heads * v_head, t)
    y = lax.dot_general(oT, wo_ref[...], (((0,), (0,)), ((), ())), preferred_element_type=F32)
    o_ref[...] = x_ref[...] + y


def _mla_mixer(x3d, positions, gain, w_a, q_norm, w_uq, kv_norm, w_ukv, q_gain, k_gain, w_o):
    b, s, d = x3d.shape
    q_lora, kv_lora = q_norm.shape[-1], kv_norm.shape[-1]
    qk_head = q_gain.shape[-1]
    heads = w_uq.shape[1] // qk_head
    v_head = w_o.shape[0] // heads
    nope = w_ukv.shape[1] // heads - v_head
    rope = qk_head - nope
    half = rope // 2
    dk = -(-qk_head // V7X_LANES) * V7X_LANES
    dv = v_head + V7X_BF16_SUBLANES
    t, kt = ATTN_TILE, ATTN_KEY_TILE
    nt = s // t
    assert w_a.shape[1] == q_lora + kv_lora + rope and s % t == 0 and t % kt == 0

    inv_freq = 1.0 / (ROPE_THETA ** (jnp.arange(0, rope, 2, dtype=F32) / rope))
    waT = w_a.T.astype(BF16)
    wuqT = w_uq.T.astype(BF16)
    wukvT = w_ukv.T.astype(BF16)

    proj_est = (2 * t * d * 4 + (w_a.size + w_uq.size + w_ukv.size) * 2
                + 2 * heads * (2 * dk + dv) * t * 2
                + (w_a.shape[1] + w_uq.shape[1] + w_ukv.shape[1]) * t * 4 * 2)
    qT, k, vT = pl.pallas_call(
        functools.partial(_mla_proj_kernel, ts=t, heads=heads, q_lora=q_lora, kv_lora=kv_lora,
                          nope=nope, rope=rope, v_head=v_head, dk=dk, dv=dv,
                          q_scale=float(qk_head ** -0.5 * math.log2(math.e))),
        grid=(b, nt),
        in_specs=[
            pl.BlockSpec((None, t, d), lambda i, j: (i, j, 0)),
            pl.BlockSpec((None, 1, t), lambda i, j: (i, 0, j)),
            _resident((1, d)),
            _resident((half, 1)),
            _resident(waT.shape),
            _resident((q_lora, 1)),
            _resident(wuqT.shape),
            _resident((kv_lora, 1)),
            _resident(wukvT.shape),
            _resident((qk_head, 1)),
            _resident((qk_head, 1)),
        ],
        out_specs=[
            pl.BlockSpec((None, heads, None, dk, t), lambda i, j: (i, 0, j, 0, 0)),
            pl.BlockSpec((None, heads, t, dk), lambda i, j: (i, 0, j, 0)),
            pl.BlockSpec((None, heads, None, dv, t), lambda i, j: (i, 0, j, 0, 0)),
        ],
        out_shape=[
            jax.ShapeDtypeStruct((b, heads, nt, dk, t), BF16),
            jax.ShapeDtypeStruct((b, heads, s, dk), BF16),
            jax.ShapeDtypeStruct((b, heads, nt, dv, t), BF16),
        ],
        compiler_params=pltpu.CompilerParams(
            dimension_semantics=("arbitrary", "arbitrary"), vmem_limit_bytes=_vmem_limit(proj_est)),
        name="mla_proj",
    )(x3d, positions.reshape(b, 1, s), gain, inv_freq.reshape(half, 1), waT,
      q_norm.reshape(q_lora, 1), wuqT, kv_norm.reshape(kv_lora, 1), wukvT,
      q_gain.reshape(qk_head, 1), k_gain.reshape(qk_head, 1))

    attn_est = 2 * (s * dk + s * dk + dv * s + v_head * s) * 2 + 10 * kt * t * 4
    oT = pl.pallas_call(
        functools.partial(_attn_kernel, t=t, kt=kt, v_head=v_head),
        grid=(b, heads),
        in_specs=[
            pl.BlockSpec((None, None, nt, dk, t), lambda i, h: (i, h, 0, 0, 0)),
            pl.BlockSpec((None, None, s, dk), lambda i, h: (i, h, 0, 0)),
            pl.BlockSpec((None, None, nt, dv, t), lambda i, h: (i, h, 0, 0, 0)),
        ],
        out_specs=pl.BlockSpec((None, None, nt, v_head, t), lambda i, h: (i, h, 0, 0, 0)),
        out_shape=jax.ShapeDtypeStruct((b, heads, nt, v_head, t), BF16),
        scratch_shapes=[pltpu.VMEM((2 * (t // kt), kt, t + V7X_LANES), F32), pltpu.VMEM((1, t), F32)],
        compiler_params=pltpu.CompilerParams(
            dimension_semantics=("arbitrary", "arbitrary"), vmem_limit_bytes=_vmem_limit(attn_est)),
        name="mla_attn",
    )(qT, k, vT)

    out_est = 4 * t * d * 4 + 2 * heads * v_head * t * 2 + w_o.size * 2 + 2 * t * d * 4
    return pl.pallas_call(
        _mla_out_kernel,
        grid=(b, nt),
        in_specs=[
            pl.BlockSpec((None, t, d), lambda i, j: (i, j, 0)),
            pl.BlockSpec((None, heads, None, v_head, t), lambda i, j: (i, 0, j, 0, 0)),
            _resident(w_o.shape),
        ],
        out_specs=pl.BlockSpec((None, t, d), lambda i, j: (i, j, 0)),
        out_shape=jax.ShapeDtypeStruct((b, s, d), F32),
        compiler_params=pltpu.CompilerParams(
            dimension_semantics=("arbitrary", "arbitrary"), vmem_limit_bytes=_vmem_limit(out_est)),
        name="mla_out",
    )(x3d, oT, w_o.astype(BF16))


def kernel(x, positions, norm_g, ffn_gate, ffn_up, ffn_down, conv_w_in, conv_k, conv_w_out, mla_w_a, mla_q_norm, mla_w_uq, mla_kv_norm, mla_w_ukv, mla_q_gain, mla_k_gain, mla_w_o, sg_w_in, sg_v_norm, sg_w_s, sg_b, sg_w_out):
    b, s, d = x.shape
    depth = norm_g.shape[0]
    m = b * s
    assert m % TOKEN_TILE == 0 and s % TOKEN_TILE == 0

    ffn_w = [(ffn_gate[0, 0].astype(BF16), ffn_up[0, 0].astype(BF16), ffn_down[0, 0].astype(BF16))]

    def ffn(x3d, i, j, gi):
        nxt = (i, j + 1) if j == 0 else (i + 1, 0)
        next_f32 = (ffn_gate, ffn_up, ffn_down, nxt) if nxt[0] < depth else None
        out, cast = _ffn(x3d.reshape(m, d), norm_g[i, gi].reshape(1, d), *ffn_w[0], next_f32=next_f32)
        ffn_w[0] = cast
        return out.reshape(b, s, d)

    ia = ib = ic = 0
    for i in range(depth):
        x = ffn(x, i, 0, 0)
        gain = norm_g[i, 1].reshape(1, d)
        kind = i % N_MIXERS
        if kind == 0:
            x = _conv_mixer(x, gain, conv_w_in[ia].astype(BF16), conv_k[ia], conv_w_out[ia].astype(BF16))
            ia += 1
        elif kind == 1:
            x = _mla_mixer(x, positions, gain, mla_w_a[ib], mla_q_norm[ib], mla_w_uq[ib], mla_kv_norm[ib],
                           mla_w_ukv[ib], mla_q_gain[ib], mla_k_gain[ib], mla_w_o[ib])
            ib += 1
        else:
            groups, chunk = sg_w_s.shape[1], sg_w_s.shape[2]
            e = sg_w_out.shape[1]
            bias_full = jnp.repeat(sg_b[ic].T, e // groups, axis=1)
            x = _sg_mixer(x.reshape(m, d), gain, sg_w_in[ic].astype(BF16), sg_v_norm[ic].reshape(1, e),
                          sg_w_s[ic], bias_full, sg_w_out[ic].astype(BF16)).reshape(b, s, d)
            ic += 1
        x = ffn(x, i, 1, 2)
    return x
```

```python
import functools
import math

import jax
import jax.numpy as jnp
from jax import lax
from jax.experimental import pallas as pl
from jax.experimental.pallas import tpu as pltpu

EPS = 1e-6
ROPE_THETA = 10000.0
N_MIXERS = 3

F32 = jnp.float32
BF16 = jnp.bfloat16

V7X_LANES = 128
V7X_BF16_SUBLANES = 16
V7X_MXU_DIM = 256
V7X_VMEM_BYTES = 64 * 1024 * 1024

TOKEN_TILE = 512
ATTN_TILE = 512
ATTN_KEY_TILE = 256
FFN_CHUNK = 2 * V7X_MXU_DIM


def _vmem_limit(estimate_bytes):
    return int(min(estimate_bytes * 3 // 2 + (8 << 20), V7X_VMEM_BYTES - (6 << 20)))


def _resident(shape):
    nd = len(shape)
    return pl.BlockSpec(shape, lambda *_: (0,) * nd, pipeline_mode=pl.Buffered(1))


def _rms_rows(x, gain):
    ms = jnp.mean(x * x, axis=-1, keepdims=True)
    return x * lax.rsqrt(ms + EPS) * gain


def _ffn_kernel(x_ref, g_ref, wg_ref, wu_ref, wd_ref, *rest, chunks, n_cast):
    o_ref, act_ref = rest[n_cast], rest[-1]
    for src_ref, dst_ref in zip(rest[:n_cast], rest[n_cast + 1:-1]):
        dst_ref[...] = src_ref[...].astype(BF16)
    x = x_ref[...]
    hn = _rms_rows(x, g_ref[...]).astype(BF16)
    for c0, cw in chunks:
        gate = jnp.dot(hn, wg_ref[:, c0:c0 + cw], preferred_element_type=F32)
        up = jnp.dot(hn, wu_ref[:, c0:c0 + cw], preferred_element_type=F32)
        act_ref[:, c0:c0 + cw] = (gate * jax.nn.sigmoid(gate) * up).astype(BF16)
    y = jnp.dot(act_ref[...], wd_ref[...], preferred_element_type=F32)
    o_ref[...] = x + 0.5 * y


def _ffn(x2d, gain, w_gate, w_up, w_down, next_f32=None):
    m, d = x2d.shape
    f = w_gate.shape[1]
    tm = 2 * TOKEN_TILE
    steps = m // tm
    chunks = tuple((c0, min(FFN_CHUNK, f - c0)) for c0 in range(0, f, FFN_CHUNK))
    est = 4 * tm * d * 4 + 3 * d * f * 2 + tm * f * 2 + 4 * tm * FFN_CHUNK * 4 + 2 * tm * d * 4
    in_specs = [
        pl.BlockSpec((tm, d), lambda i: (i, 0)),
        _resident((1, d)),
        _resident((d, f)),
        _resident((d, f)),
        _resident((f, d)),
    ]
    out_specs = [pl.BlockSpec((tm, d), lambda i: (i, 0))]
    out_shape = [jax.ShapeDtypeStruct((m, d), F32)]
    operands = [x2d, gain, w_gate, w_up, w_down]
    n_cast = 0
    if next_f32 is not None:
        *stacked, (li, lj) = next_f32
        n_cast = len(stacked)
        for w in stacked:
            rows, cols = w.shape[2:]
            rb = next(r for r in range(V7X_BF16_SUBLANES, rows + 1, V7X_BF16_SUBLANES)
                      if rows % r == 0 and rows // r <= steps)
            last = rows // rb - 1
            in_specs.append(pl.BlockSpec(
                (None, None, rb, cols), lambda i, last=last: (li, lj, jnp.minimum(i, last), 0)))
            out_specs.append(pl.BlockSpec((rb, cols), lambda i, last=last: (jnp.minimum(i, last), 0)))
            out_shape.append(jax.ShapeDtypeStruct((rows, cols), BF16))
            operands.append(w)
            est += 2 * rb * cols * 6
    out, *cast = pl.pallas_call(
        functools.partial(_ffn_kernel, chunks=chunks, n_cast=n_cast),
        grid=(steps,),
        in_specs=in_specs,
        out_specs=out_specs,
        out_shape=out_shape,
        scratch_shapes=[pltpu.VMEM((tm, f), BF16)],
        compiler_params=pltpu.CompilerParams(
            dimension_semantics=("arbitrary",), vmem_limit_bytes=_vmem_limit(est)),
        name="ffn",
    )(*operands)
    return out, (tuple(cast) if cast else None)


def _conv_kernel(x_ref, g_ref, win_ref, ck_ref, wout_ref, o_ref, z_ref, *, ts, d, halo, n_sub):
    @pl.when(pl.program_id(1) == 0)
    def _():
        z_ref[0:halo, :] = jnp.zeros((halo, d), F32)

    sub = ts // n_sub
    b_gates = []
    for i in range(n_sub):
        hn = _rms_rows(x_ref[i * sub:(i + 1) * sub, :], g_ref[...]).astype(BF16)
        b_gates.append(jnp.dot(hn, win_ref[:, 0:d], preferred_element_type=F32))
        c_gate = jnp.dot(hn, win_ref[:, d:2 * d], preferred_element_type=F32)
        h = jnp.dot(hn, win_ref[:, 2 * d:3 * d], preferred_element_type=F32)
        z_ref[halo + i * sub:halo + (i + 1) * sub, :] = c_gate * h
    ck = ck_ref[...]
    for i in range(n_sub):
        r0 = halo + i * sub
        conv = (ck[0:1, :] * z_ref[r0 - 2:r0 - 2 + sub, :]
                + ck[1:2, :] * z_ref[r0 - 1:r0 - 1 + sub, :]
                + ck[2:3, :] * z_ref[r0:r0 + sub, :])
        y = jnp.dot((b_gates[i] * conv).astype(BF16), wout_ref[...], preferred_element_type=F32)
        o_ref[i * sub:(i + 1) * sub, :] = x_ref[i * sub:(i + 1) * sub, :] + y
    z_ref[0:halo, :] = z_ref[ts:ts + halo, :]


def _conv_mixer(x3d, gain, w_in_all, conv_k, w_out_all, layer):
    b, s, d = x3d.shape
    n_sub = 4
    ts = 2 * TOKEN_TILE

    def layer_weight(shape):
        return pl.BlockSpec((None,) + shape, lambda *_: (layer, 0, 0), pipeline_mode=pl.Buffered(1))

    halo = 8
    assert conv_k.shape[0] == 3 and halo >= conv_k.shape[0] - 1
    est = 4 * ts * d * 4 + 4 * d * d * 2 + (ts + halo) * d * 4 + 5 * ts * d * 4
    return pl.pallas_call(
        functools.partial(_conv_kernel, ts=ts, d=d, halo=halo, n_sub=n_sub),
        grid=(b, s // ts),
        in_specs=[
            pl.BlockSpec((None, ts, d), lambda i, j: (i, j, 0)),
            _resident((1, d)),
            layer_weight((d, 3 * d)),
            _resident((3, d)),
            layer_weight((d, d)),
        ],
        out_specs=pl.BlockSpec((None, ts, d), lambda i, j: (i, j, 0)),
        out_shape=jax.ShapeDtypeStruct((b, s, d), F32),
        scratch_shapes=[pltpu.VMEM((ts + halo, d), F32)],
        compiler_params=pltpu.CompilerParams(
            dimension_semantics=("arbitrary", "arbitrary"), vmem_limit_bytes=_vmem_limit(est)),
        name="conv_mixer",
    )(x3d, gain, w_in_all, conv_k, w_out_all)


def _sg_kernel(x_ref, g_ref, win_ref, vn_ref, ws_ref, bias_ref, wout_ref, o_ref, gated_ref,
               *, ts, e, groups, chunk, n_sub):
    sub = ts // n_sub
    us, vs = [], []
    for i in range(n_sub):
        hn = _rms_rows(x_ref[i * sub:(i + 1) * sub, :], g_ref[...]).astype(BF16)
        us.append(jax.nn.gelu(jnp.dot(hn, win_ref[:, 0:e], preferred_element_type=F32)))
        v = jax.nn.gelu(jnp.dot(hn, win_ref[:, e:2 * e], preferred_element_type=F32))
        vs.append(_rms_rows(v, vn_ref[...]).astype(BF16))
    cg = e // groups
    row = lax.broadcasted_iota(jnp.int32, (chunk, chunk), 0)
    col = lax.broadcasted_iota(jnp.int32, (chunk, chunk), 1)
    causal = col <= row
    w_masked = [jnp.where(causal, ws_ref[gi], 0.0).astype(BF16) for gi in range(groups)]
    for i in range(n_sub):
        u, v = us[i], vs[i]
        for gi in range(groups):
            for ci in range(sub // chunk):
                rows = slice(ci * chunk, (ci + 1) * chunk)
                cols = slice(gi * cg, (gi + 1) * cg)
                mixed = jnp.dot(w_masked[gi], v[rows, cols], preferred_element_type=F32) + bias_ref[:, cols]
                gated_ref[i * sub + ci * chunk:i * sub + (ci + 1) * chunk, cols] = (u[rows, cols] * mixed).astype(BF16)
        y = jnp.dot(gated_ref[i * sub:(i + 1) * sub, :], wout_ref[...], preferred_element_type=F32)
        o_ref[i * sub:(i + 1) * sub, :] = x_ref[i * sub:(i + 1) * sub, :] + y


def _sg_mixer(x2d, gain, w_in, v_norm, w_s, bias_full, w_out):
    m, d = x2d.shape
    e = w_out.shape[0]
    groups, chunk, _ = w_s.shape
    n_sub = 2
    ts = n_sub * TOKEN_TILE
    assert TOKEN_TILE % chunk == 0 and (e // groups) % V7X_LANES == 0
    est = 4 * ts * d * 4 + (d * 2 * e + e * d) * 2 + groups * chunk * chunk * 4 + 4 * ts * e * 4
    return pl.pallas_call(
        functools.partial(_sg_kernel, ts=ts, e=e, groups=groups, chunk=chunk, n_sub=n_sub),
        grid=(m // ts,),
        in_specs=[
            pl.BlockSpec((ts, d), lambda i: (i, 0)),
            _resident((1, d)),
            _resident((d, 2 * e)),
            _resident((1, e)),
            _resident((groups, chunk, chunk)),
            _resident((chunk, e)),
            _resident((e, d)),
        ],
        out_specs=pl.BlockSpec((ts, d), lambda i: (i, 0)),
        out_shape=jax.ShapeDtypeStruct((m, d), F32),
        scratch_shapes=[pltpu.VMEM((ts, e), BF16)],
        compiler_params=pltpu.CompilerParams(
            dimension_semantics=("arbitrary",), vmem_limit_bytes=_vmem_limit(est)),
        name="sg_mixer",
    )(x2d, gain, w_in, v_norm, w_s, bias_full, w_out)


def _mla_proj_kernel(x_ref, pos_ref, g_ref, inv_freq_ref, waT_ref, qn_ref, wuqT_ref, kvn_ref, wukvT_ref,
                     qg_ref, kg_ref, qT_ref, k_ref, vT_ref,
                     *, ts, heads, q_lora, kv_lora, nope, rope, v_head, dk, dv, q_scale):
    half = rope // 2
    qk_head = nope + rope
    hn = _rms_rows(x_ref[...], g_ref[...]).astype(BF16)
    aT = lax.dot_general(waT_ref[...], hn, (((1,), (1,)), ((), ())), preferred_element_type=F32)
    c_q = aT[0:q_lora, :]
    c_kv = aT[q_lora:q_lora + kv_lora, :]
    k_pe = aT[q_lora + kv_lora:q_lora + kv_lora + rope, :]

    def rms_cols(t, gain_col):
        ms = jnp.mean(t * t, axis=0, keepdims=True)
        return t * lax.rsqrt(ms + EPS) * gain_col

    cqn = rms_cols(c_q, qn_ref[...]).astype(BF16)
    ckvn = rms_cols(c_kv, kvn_ref[...]).astype(BF16)
    qT = jnp.dot(wuqT_ref[...], cqn, preferred_element_type=F32)
    kvT = jnp.dot(wukvT_ref[...], ckvn, preferred_element_type=F32)

    ang = inv_freq_ref[...] * pos_ref[...].astype(F32)
    cos = jnp.cos(ang)
    sin = jnp.sin(ang)

    def rotate(t):
        t1, t2 = t[0:half, :], t[half:rope, :]
        return t1 * cos - t2 * sin, t2 * cos + t1 * sin

    q_gain = qg_ref[...] * q_scale
    k_gain = kg_ref[...]
    pe_sq = jnp.sum(k_pe * k_pe, axis=0, keepdims=True)
    zeros_q = jnp.zeros((dk - qk_head, ts), BF16)
    ones_row = (lax.broadcasted_iota(jnp.int32, (dv - v_head, ts), 0) == 0).astype(BF16)
    for h in range(heads):
        qh = qT[h * qk_head:(h + 1) * qk_head, :]
        qh = rms_cols(qh, q_gain)
        r1, r2 = rotate(qh[nope:qk_head, :])
        qT_ref[h, 0:nope, :] = qh[0:nope, :].astype(BF16)
        qT_ref[h, nope:nope + half, :] = r1.astype(BF16)
        qT_ref[h, nope + half:qk_head, :] = r2.astype(BF16)
        qT_ref[h, qk_head:dk, :] = zeros_q

        base = h * (nope + v_head)
        kn = kvT[base:base + nope, :]
        ms = (jnp.sum(kn * kn, axis=0, keepdims=True) + pe_sq) * (1.0 / qk_head)
        rs = lax.rsqrt(ms + EPS)
        kn = kn * rs * k_gain[0:nope, :]
        r1, r2 = rotate(k_pe * rs * k_gain[nope:qk_head, :])
        k_fm = jnp.concatenate([kn, r1, r2, jnp.zeros((dk - qk_head, ts), F32)], axis=0)
        k_ref[h] = k_fm.T.astype(BF16)

        vT_ref[h, 0:v_head, :] = kvT[base + nope:base + nope + v_head, :].astype(BF16)
        vT_ref[h, v_head:dv, :] = ones_row


def _attn_kernel(qT_ref, k_ref, vT_ref, oT_ref, s_ref, mx_ref, *, t, kt, v_head):
    n_sub = t // kt
    n_tiles = qT_ref.shape[0]

    def produce(qT, g, wr):
        mx = None
        for n in range(n_sub):
            start = (g * n_sub + n) * kt
            if not isinstance(start, int):
                start = pl.multiple_of(start, kt)
            sc = jnp.dot(k_ref[pl.ds(start, kt), :], qT, preferred_element_type=F32)
            s_ref[wr + n, :, 0:t] = sc
            sub_max = jnp.max(sc, axis=0, keepdims=True)
            mx = sub_max if mx is None else jnp.maximum(mx, sub_max)
        return mx

    mx_ref[...] = produce(qT_ref[0], 0, 0)

    def q_tile(qi, _):
        _attn_query_tile(qi, qT_ref, vT_ref, oT_ref, s_ref, mx_ref, produce,
                         t=t, kt=kt, v_head=v_head, n_sub=n_sub, n_tiles=n_tiles)
        return 0

    lax.fori_loop(0, n_tiles, q_tile, 0)


def _attn_query_tile(qi, qT_ref, vT_ref, oT_ref, s_ref, mx_ref, produce, *, t, kt, v_head, n_sub, n_tiles):
    qT = qT_ref[qi]

    def group(g, rd, wr, m, mx, acc):
        mx_next = produce(qT, g + 1, wr)
        m_new = jnp.maximum(m, mx)
        alpha = jnp.exp2(m - m_new)
        p = jnp.concatenate([jnp.exp2(s_ref[rd + n, :, 0:t] - m_new).astype(BF16) for n in range(n_sub)], axis=0)
        acc = alpha * acc + jnp.dot(vT_ref[g], p, preferred_element_type=F32)
        return m_new, mx_next, acc

    def finish(rd, m, mx, acc):
        del mx
        masked = []
        for n in range(n_sub):
            s = s_ref[rd + n, :, n * kt:t]
            k_pos = lax.broadcasted_iota(jnp.int32, s.shape, 0)
            q_pos = lax.broadcasted_iota(jnp.int32, s.shape, 1)
            masked.append(jnp.where(k_pos <= q_pos, s, -jnp.inf))
        vT = vT_ref[qi]
        mx_ref[...] = produce(qT_ref[jnp.minimum(qi + 1, n_tiles - 1)], 0, 0)
        for n in range(n_sub):
            q_lo = n * kt
            s, m_old, acc_old = masked[n], m[:, q_lo:], acc[:, q_lo:]
            m_hi = jnp.maximum(m_old, jnp.max(s, axis=0, keepdims=True))
            p = jnp.exp2(s - m_hi).astype(BF16)
            acc_hi = jnp.exp2(m_old - m_hi) * acc_old + jnp.dot(
                vT[:, q_lo:q_lo + kt], p, preferred_element_type=F32)
            if n == 0:
                m, acc = m_hi, acc_hi
            else:
                m = jnp.concatenate([m[:, :q_lo], m_hi], axis=1)
                acc = jnp.concatenate([acc[:, :q_lo], acc_hi], axis=1)
        oT_ref[qi] = (acc[0:v_head, :] / acc[v_head:v_head + 1, :]).astype(BF16)

    def pair(gg, carry):
        carry = group(2 * gg, 0, n_sub, *carry)
        return group(2 * gg + 1, n_sub, 0, *carry)

    m = jnp.full((1, t), -jnp.inf, F32)
    acc = jnp.zeros((vT_ref.shape[1], t), F32)
    carry = lax.fori_loop(0, qi // 2, pair, (m, mx_ref[...], acc))

    @pl.when(qi % 2 == 0)
    def _():
        finish(0, *carry)

    @pl.when(qi % 2 == 1)
    def _():
        finish(n_sub, *group(qi - 1, 0, n_sub, *carry))


def _mla_out_kernel(x_ref, oT_ref, wo_ref, o_ref):
    heads, v_head, t = oT_ref.shape
    oT = oT_ref[...].reshape(heads * v_head, t)
    y = lax.dot_general(oT, wo_ref[...], (((0,), (0,)), ((), ())), preferred_element_type=F32)
    o_ref[...] = x_ref[...] + y


def _mla_mixer(x3d, positions, gain, w_a, q_norm, w_uq, kv_norm, w_ukv, q_gain, k_gain, w_o):
    b, s, d = x3d.shape
    q_lora, kv_lora = q_norm.shape[-1], kv_norm.shape[-1]
    qk_head = q_gain.shape[-1]
    heads = w_uq.shape[1] // qk_head
    v_head = w_o.shape[0] // heads
    nope = w_ukv.shape[1] // heads - v_head
    rope = qk_head - nope
    half = rope // 2
    dk = -(-qk_head // V7X_LANES) * V7X_LANES
    dv = v_head + V7X_BF16_SUBLANES
    t, kt = ATTN_TILE, ATTN_KEY_TILE
    nt = s // t
    assert w_a.shape[1] == q_lora + kv_lora + rope and s % t == 0 and t % kt == 0

    inv_freq = 1.0 / (ROPE_THETA ** (jnp.arange(0, rope, 2, dtype=F32) / rope))
    waT = w_a.T.astype(BF16)
    wuqT = w_uq.T.astype(BF16)
    wukvT = w_ukv.T.astype(BF16)

    proj_est = (2 * t * d * 4 + (w_a.size + w_uq.size + w_ukv.size) * 2
                + 2 * heads * (2 * dk + dv) * t * 2
                + (w_a.shape[1] + w_uq.shape[1] + w_ukv.shape[1]) * t * 4 * 2)
    qT, k, vT = pl.pallas_call(
        functools.partial(_mla_proj_kernel, ts=t, heads=heads, q_lora=q_lora, kv_lora=kv_lora,
                          nope=nope, rope=rope, v_head=v_head, dk=dk, dv=dv,
                          q_scale=float(qk_head ** -0.5 * math.log2(math.e))),
        grid=(b, nt),
        in_specs=[
            pl.BlockSpec((None, t, d), lambda i, j: (i, j, 0)),
            pl.BlockSpec((None, 1, t), lambda i, j: (i, 0, j)),
            _resident((1, d)),
            _resident((half, 1)),
            _resident(waT.shape),
            _resident((q_lora, 1)),
            _resident(wuqT.shape),
            _resident((kv_lora, 1)),
            _resident(wukvT.shape),
            _resident((qk_head, 1)),
            _resident((qk_head, 1)),
        ],
        out_specs=[
            pl.BlockSpec((None, heads, None, dk, t), lambda i, j: (i, 0, j, 0, 0)),
            pl.BlockSpec((None, heads, t, dk), lambda i, j: (i, 0, j, 0)),
            pl.BlockSpec((None, heads, None, dv, t), lambda i, j: (i, 0, j, 0, 0)),
        ],
        out_shape=[
            jax.ShapeDtypeStruct((b, heads, nt, dk, t), BF16),
            jax.ShapeDtypeStruct((b, heads, s, dk), BF16),
            jax.ShapeDtypeStruct((b, heads, nt, dv, t), BF16),
        ],
        compiler_params=pltpu.CompilerParams(
            dimension_semantics=("arbitrary", "arbitrary"), vmem_limit_bytes=_vmem_limit(proj_est)),
        name="mla_proj",
    )(x3d, positions.reshape(b, 1, s), gain, inv_freq.reshape(half, 1), waT,
      q_norm.reshape(q_lora, 1), wuqT, kv_norm.reshape(kv_lora, 1), wukvT,
      q_gain.reshape(qk_head, 1), k_gain.reshape(qk_head, 1))

    attn_est = 2 * (s * dk + s * dk + dv * s + v_head * s) * 2 + 10 * kt * t * 4
    oT = pl.pallas_call(
        functools.partial(_attn_kernel, t=t, kt=kt, v_head=v_head),
        grid=(b, heads),
        in_specs=[
            pl.BlockSpec((None, None, nt, dk, t), lambda i, h: (i, h, 0, 0, 0)),
            pl.BlockSpec((None, None, s, dk), lambda i, h: (i, h, 0, 0)),
            pl.BlockSpec((None, None, nt, dv, t), lambda i, h: (i, h, 0, 0, 0)),
        ],
        out_specs=pl.BlockSpec((None, None, nt, v_head, t), lambda i, h: (i, h, 0, 0, 0)),
        out_shape=jax.ShapeDtypeStruct((b, heads, nt, v_head, t), BF16),
        scratch_shapes=[pltpu.VMEM((2 * (t // kt), kt, t + V7X_LANES), F32), pltpu.VMEM((1, t), F32)],
        compiler_params=pltpu.CompilerParams(
            dimension_semantics=("arbitrary", "arbitrary"), vmem_limit_bytes=_vmem_limit(attn_est)),
        name="mla_attn",
    )(qT, k, vT)

    out_est = 4 * t * d * 4 + 2 * heads * v_head * t * 2 + w_o.size * 2 + 2 * t * d * 4
    return pl.pallas_call(
        _mla_out_kernel,
        grid=(b, nt),
        in_specs=[
            pl.BlockSpec((None, t, d), lambda i, j: (i, j, 0)),
            pl.BlockSpec((None, heads, None, v_head, t), lambda i, j: (i, 0, j, 0, 0)),
            _resident(w_o.shape),
        ],
        out_specs=pl.BlockSpec((None, t, d), lambda i, j: (i, j, 0)),
        out_shape=jax.ShapeDtypeStruct((b, s, d), F32),
        compiler_params=pltpu.CompilerParams(
            dimension_semantics=("arbitrary", "arbitrary"), vmem_limit_bytes=_vmem_limit(out_est)),
        name="mla_out",
    )(x3d, oT, w_o.astype(BF16))


def kernel(x, positions, norm_g, ffn_gate, ffn_up, ffn_down, conv_w_in, conv_k, conv_w_out, mla_w_a, mla_q_norm, mla_w_uq, mla_kv_norm, mla_w_ukv, mla_q_gain, mla_k_gain, mla_w_o, sg_w_in, sg_v_norm, sg_w_s, sg_b, sg_w_out):
    b, s, d = x.shape
    depth = norm_g.shape[0]
    m = b * s
    assert m % TOKEN_TILE == 0 and s % TOKEN_TILE == 0

    ffn_w = [(ffn_gate[0, 0].astype(BF16), ffn_up[0, 0].astype(BF16), ffn_down[0, 0].astype(BF16))]

    def ffn(x3d, i, j, gi):
        nxt = (i, j + 1) if j == 0 else (i + 1, 0)
        next_f32 = (ffn_gate, ffn_up, ffn_down, nxt) if nxt[0] < depth else None
        out, cast = _ffn(x3d.reshape(m, d), norm_g[i, gi].reshape(1, d), *ffn_w[0], next_f32=next_f32)
        ffn_w[0] = cast
        return out.reshape(b, s, d)

    ia = ib = ic = 0
    for i in range(depth):
        x = ffn(x, i, 0, 0)
        gain = norm_g[i, 1].reshape(1, d)
        kind = i % N_MIXERS
        if kind == 0:
            x = _conv_mixer(x, gain, conv_w_in.astype(BF16), conv_k[ia], conv_w_out.astype(BF16), ia)
            ia += 1
        elif kind == 1:
            x = _mla_mixer(x, positions, gain, mla_w_a[ib], mla_q_norm[ib], mla_w_uq[ib], mla_kv_norm[ib],
                           mla_w_ukv[ib], mla_q_gain[ib], mla_k_gain[ib], mla_w_o[ib])
            ib += 1
        else:
            groups, chunk = sg_w_s.shape[1], sg_w_s.shape[2]
            e = sg_w_out.shape[1]
            bias_full = jnp.repeat(sg_b[ic].T, e // groups, axis=1)
            x = _sg_mixer(x.reshape(m, d), gain, sg_w_in[ic].astype(BF16), sg_v_norm[ic].reshape(1, e),
                          sg_w_s[ic], bias_full, sg_w_out[ic].astype(BF16)).reshape(b, s, d)
            ic += 1
        x = ffn(x, i, 1, 2)
    return x
```
